```python
import math
import jax, jax.numpy as jnp
from jax import lax
import numpy as np

D_MODEL = 1024
BATCH = 4
SEQ = 8192
DEPTH = 1

GRID_W = 64
CTX_LEN = 256
EPS = 1e-6
GLA_HEADS = 4
GLA_DK = D_MODEL // (2 * GLA_HEADS)
GLA_DV = D_MODEL // GLA_HEADS
GLA_WIDTH = GLA_HEADS * GLA_DV
GLA_RANK = 16
GLA_TAU = 16.0
GLA_CHUNK = 64
DIFF_DK = 64
DIFF_DV = 2 * DIFF_DK
DIFF_HEADS = D_MODEL // DIFF_DV
DIFF_WIDTH = DIFF_HEADS * DIFF_DV
Q_BLOCK = 128
ROPE_BASE = 10000.0
ROPE_AXIS_DIM = DIFF_DK // 2
IN_SIZES = (GLA_HEADS * GLA_DK, GLA_HEADS * GLA_DK, GLA_WIDTH, GLA_WIDTH, 2 * GLA_RANK,
            DIFF_HEADS * 2 * DIFF_DK, DIFF_HEADS * 2 * DIFF_DK, DIFF_WIDTH, DIFF_WIDTH,
            D_MODEL, D_MODEL)
IN_WIDTH = sum(IN_SIZES)

kernel_name = 'hybrid_gla_diffattn_gated_parallel_block'


def rms_norm(t):
    tf = t.astype(jnp.float32)
    return (tf * lax.rsqrt(jnp.mean(tf * tf, axis=-1, keepdims=True) + EPS)).astype(t.dtype)


def modulate(t, shift, scale):
    return rms_norm(t) * (1 + scale) + shift


def heads(t, n):
    b_, l_, w_ = t.shape
    return t.reshape(b_, l_, n, w_ // n).transpose(0, 2, 1, 3)


def merge_heads(t):
    b_, h_, l_, d_ = t.shape
    return t.transpose(0, 2, 1, 3).reshape(b_, l_, h_ * d_)


def rev(t):
    return jnp.flip(t, axis=2)


def project(h, w_in):
    p = h @ w_in
    return jnp.split(p, np.cumsum(IN_SIZES)[:-1].tolist(), axis=-1)


def axial_rope_tables(n_tokens, dtype):
    rows = n_tokens // GRID_W
    row = jnp.repeat(jnp.arange(rows, dtype=jnp.float32), GRID_W)
    col = jnp.tile(jnp.arange(GRID_W, dtype=jnp.float32), rows)
    inv_freq = ROPE_BASE ** (-jnp.arange(0, ROPE_AXIS_DIM, 2, dtype=jnp.float32) / ROPE_AXIS_DIM)
    ang_r = row[:, None] * inv_freq
    ang_c = col[:, None] * inv_freq
    two = lambda a: jnp.concatenate([a, a], axis=-1).astype(dtype)
    return (two(jnp.cos(ang_r)), two(jnp.sin(ang_r)), two(jnp.cos(ang_c)), two(jnp.sin(ang_c)))


def rope_axis(t, cos, sin):
    t1, t2 = jnp.split(t, 2, axis=-1)
    return t * cos + jnp.concatenate([-t2, t1], axis=-1) * sin


def apply_rope(t, tabs):
    cr, sr, cc, sc = tabs
    tr, tc = jnp.split(t, 2, axis=-1)
    return jnp.concatenate([rope_axis(tr, cr, sr), rope_axis(tc, cc, sc)], axis=-1)


def gla_streams(gq, gk, gv, g_lr, w_dec, b_dec):
    q = heads(gq, GLA_HEADS) * GLA_DK ** -0.5
    k = heads(gk, GLA_HEADS)
    v = heads(gv, GLA_HEADS)
    lr_f, lr_b = jnp.split(g_lr, 2, axis=-1)
    la_f = jax.nn.log_sigmoid((lr_f @ w_dec[0] + b_dec[0]).astype(jnp.float32)) / GLA_TAU
    la_b = jax.nn.log_sigmoid((lr_b @ w_dec[1] + b_dec[1]).astype(jnp.float32)) / GLA_TAU
    return q, k, v, heads(la_f, GLA_HEADS), heads(la_b, GLA_HEADS)


def gla_final_state(k, v, log_a):
    cum = jnp.cumsum(log_a.astype(jnp.float32), axis=2)
    k_dec = k.astype(jnp.float32) * jnp.exp(cum[:, :, -1:] - cum)
    return jnp.einsum('bhld,bhlv->bhdv', k_dec, v.astype(jnp.float32))


def gla_chunked(q, k, v, log_a, s0):
    b_, h_, l_, _ = q.shape
    dv = v.shape[-1]
    n = l_ // GLA_CHUNK

    def chunks(t):
        return t.astype(jnp.float32).reshape(b_, h_, n, GLA_CHUNK, t.shape[-1])

    q, k, v, log_a = chunks(q), chunks(k), chunks(v), chunks(log_a)
    cum = jnp.cumsum(log_a, axis=3)
    ref = cum[:, :, :, GLA_CHUNK // 2 - 1:GLA_CHUNK // 2]
    scores = jnp.einsum('bhnid,bhnjd->bhnij', q * jnp.exp(cum - ref), k * jnp.exp(ref - cum))
    tri = jnp.tril(jnp.ones((GLA_CHUNK, GLA_CHUNK), dtype=bool))
    o_intra = jnp.einsum('bhnij,bhnjv->bhniv', jnp.where(tri, scores, 0.0), v)
    last = cum[:, :, :, -1:]
    q_inter = q * jnp.exp(cum)
    k_state = k * jnp.exp(last - cum)
    decay = jnp.exp(last[:, :, :, 0])

    def step(s, inp):
        qc, kc, vc, dc = inp
        o = jnp.einsum('bhid,bhdv->bhiv', qc, s)
        s = dc[..., None] * s + jnp.einsum('bhid,bhiv->bhdv', kc, vc)
        return s, o

    front = lambda t: jnp.moveaxis(t, 2, 0)
    _, o_inter = lax.scan(step, s0.astype(jnp.float32),
                          (front(q_inter), front(k_state), front(v), front(decay)))
    o = o_intra + jnp.moveaxis(o_inter, 0, 2)
    return o.reshape(b_, h_, l_, dv)


def bidir_gla(q, k, v, la_f, la_b, s_f, s_b):
    fwd = gla_chunked(q, k, v, la_f, s_f)
    bwd = rev(gla_chunked(rev(q), rev(k), rev(v), rev(la_b), s_b))
    return (fwd + bwd).astype(v.dtype)


def diff_streams(dq, dk, dv, q_gain, k_gain):
    q1, q2 = jnp.split(heads(dq, DIFF_HEADS), 2, axis=-1)
    k1, k2 = jnp.split(heads(dk, DIFF_HEADS), 2, axis=-1)
    v = heads(dv, DIFF_HEADS)
    return (rms_norm(q1) * q_gain, rms_norm(q2) * q_gain,
            rms_norm(k1) * k_gain, rms_norm(k2) * k_gain, v)


def diff_lambda_value(lam_params, lam_init):
    lq1, lk1, lq2, lk2 = lam_params.astype(jnp.float32)
    return jnp.exp(jnp.sum(lq1 * lk1)) - jnp.exp(jnp.sum(lq2 * lk2)) + lam_init


def diff_attention(q1, q2, k1, k2, v, lam):
    b_, h_, lq, d = q1.shape
    nb = lq // Q_BLOCK
    scale = d ** -0.5

    def blocks(t):
        return jnp.moveaxis(t.reshape(b_, h_, nb, Q_BLOCK, d), 2, 0)

    def one_block(qs):
        qb1, qb2 = qs
        s1 = jnp.einsum('bhqd,bhkd->bhqk', qb1, k1).astype(jnp.float32) * scale
        s2 = jnp.einsum('bhqd,bhkd->bhqk', qb2, k2).astype(jnp.float32) * scale
        a = jax.nn.softmax(s1, axis=-1) - lam * jax.nn.softmax(s2, axis=-1)
        return jnp.einsum('bhqk,bhkv->bhqv', a.astype(v.dtype), v)

    o = lax.map(one_block, (blocks(q1), blocks(q2)))
    return jnp.moveaxis(o, 0, 2).reshape(b_, h_, lq, v.shape[-1])


def combine(o_gla, gla_gate, o_diff, diff_gate, m_gla, m_diff,
            gla_gain, diff_gain, lam_init, w_br_gla, w_br_diff, w_out):
    a = merge_heads(rms_norm(o_gla) * gla_gain) * jax.nn.silu(gla_gate)
    b = merge_heads(rms_norm(o_diff) * diff_gain * (1 - lam_init)) * jax.nn.silu(diff_gate)
    y = jax.nn.sigmoid(m_gla) * (a @ w_br_gla) + jax.nn.sigmoid(m_diff) * (b @ w_br_diff)
    return y @ w_out


def setup_inputs(seed: int = 0) -> dict:
    key = jax.random.key(seed)
    ks = jax.random.split(key, 17)
    nrm = lambda k, shape, s: jax.random.normal(k, shape, jnp.float32) * s
    return {
        'x': nrm(ks[0], (BATCH, SEQ, D_MODEL), 1.0),
        'c': nrm(ks[1], (BATCH, D_MODEL), 1.0),
        'ctx': nrm(ks[2], (BATCH, CTX_LEN, D_MODEL), 1.0),
        'c_ctx': nrm(ks[3], (D_MODEL,), 1.0),
        'w_ada': nrm(ks[4], (DEPTH, D_MODEL, 3 * D_MODEL), D_MODEL ** -0.5),
        'b_ada': nrm(ks[5], (DEPTH, 3 * D_MODEL), 0.02),
        'w_in': nrm(ks[6], (DEPTH, D_MODEL, IN_WIDTH), D_MODEL ** -0.5),
        'gla_w_decay': nrm(ks[7], (DEPTH, 2, GLA_RANK, GLA_HEADS * GLA_DK), GLA_RANK ** -0.5),
        'gla_b_decay': nrm(ks[8], (DEPTH, 2, GLA_HEADS * GLA_DK), 0.1),
        'gla_norm': 1.0 + nrm(ks[9], (DEPTH, GLA_DV), 0.02),
        'diff_q_norm': 1.0 + nrm(ks[10], (DEPTH, DIFF_DK), 0.02),
        'diff_k_norm': 1.0 + nrm(ks[11], (DEPTH, DIFF_DK), 0.02),
        'diff_lambda': nrm(ks[12], (DEPTH, 4, DIFF_DK), 0.1),
        'diff_norm': 1.0 + nrm(ks[13], (DEPTH, DIFF_DV), 0.02),
        'w_br_gla': nrm(ks[14], (DEPTH, GLA_WIDTH, D_MODEL), GLA_WIDTH ** -0.5),
        'w_br_diff': nrm(ks[15], (DEPTH, DIFF_WIDTH, D_MODEL), DIFF_WIDTH ** -0.5),
        'w_out': nrm(ks[16], (DEPTH, D_MODEL, D_MODEL), D_MODEL ** -0.5),
    }


def reference(x, c, ctx, c_ctx, w_ada, b_ada, w_in, gla_w_decay, gla_b_decay, gla_norm,
              diff_q_norm, diff_k_norm, diff_lambda, diff_norm, w_br_gla, w_br_diff, w_out):
    rope = axial_rope_tables(x.shape[1], x.dtype)
    for layer in range(DEPTH):
        lam_init = 0.8 - 0.6 * math.exp(-0.3 * layer)
        mod_x = jax.nn.silu(c) @ w_ada[layer] + b_ada[layer]
        mod_c = jax.nn.silu(c_ctx) @ w_ada[layer] + b_ada[layer]
        sh_x, sc_x, g_x = jnp.split(mod_x[:, None, :], 3, axis=-1)
        sh_c, sc_c, g_c = jnp.split(mod_c, 3, axis=-1)
        pl = project(modulate(x, sh_x, sc_x), w_in[layer])
        pc = project(modulate(ctx, sh_c, sc_c), w_in[layer])

        q, k, v, la_f, la_b = gla_streams(pl[0], pl[1], pl[2], pl[4],
                                          gla_w_decay[layer], gla_b_decay[layer])
        qc, kc, vc, la_fc, la_bc = gla_streams(pc[0], pc[1], pc[2], pc[4],
                                               gla_w_decay[layer], gla_b_decay[layer])
        s_f = gla_final_state(kc, vc, la_fc)
        s_b = gla_final_state(rev(kc), rev(vc), rev(la_bc))
        o_gla = bidir_gla(q, k, v, la_f, la_b, s_f, s_b)

        q1, q2, k1, k2, dv = diff_streams(pl[5], pl[6], pl[7], diff_q_norm[layer], diff_k_norm[layer])
        q1, q2, k1, k2 = (apply_rope(q1, rope), apply_rope(q2, rope),
                          apply_rope(k1, rope), apply_rope(k2, rope))
        cq1, cq2, ck1, ck2, cv = diff_streams(pc[5], pc[6], pc[7], diff_q_norm[layer], diff_k_norm[layer])
        lam = diff_lambda_value(diff_lambda[layer], lam_init)
        o_diff = diff_attention(q1, q2,
                                jnp.concatenate([ck1, k1], axis=2),
                                jnp.concatenate([ck2, k2], axis=2),
                                jnp.concatenate([cv, dv], axis=2), lam)

        out_x = combine(o_gla, pl[3], o_diff, pl[8], pl[9], pl[10], gla_norm[layer], diff_norm[layer],
                        lam_init, w_br_gla[layer], w_br_diff[layer], w_out[layer])
        if layer < DEPTH - 1:
            zero = jnp.zeros_like(s_f)
            o_gla_c = bidir_gla(qc, kc, vc, la_fc, la_bc, zero, zero)
            o_diff_c = diff_attention(cq1, cq2, ck1, ck2, cv, lam)
            out_c = combine(o_gla_c, pc[3], o_diff_c, pc[8], pc[9], pc[10], gla_norm[layer],
                            diff_norm[layer], lam_init, w_br_gla[layer], w_br_diff[layer], w_out[layer])
            ctx = ctx + g_c * out_c
        x = x + g_x * out_x
    return x
```

```python
import functools
import math

import jax
import jax.numpy as jnp
import numpy as np
from jax import lax
from jax.experimental import pallas as pl
from jax.experimental.pallas import tpu as pltpu

F32 = jnp.float32
BF16 = jnp.bfloat16
HIGHEST = lax.Precision.HIGHEST

EPS = 1e-6
GRID_W = 64
GLA_HEADS = 4
GLA_RANK = 16
GLA_TAU = 16.0
GLA_CHUNK = 64
DIFF_DK = 64
ROPE_BASE = 10000.0
ROPE_AXIS_DIM = DIFF_DK // 2
LANES = 128
ROW_TILE = 256
VMEM_LIMIT = 56 * 1024 * 1024


def _dot(a, b):
    return jnp.dot(a, b, preferred_element_type=F32)


def _dot_nt(a, b):
    return lax.dot_general(a, b, (((1,), (1,)), ((), ())), preferred_element_type=F32)


def _dot_exact(a, b):
    return lax.dot_general(a, b, (((1,), (0,)), ((), ())), precision=HIGHEST,
                           preferred_element_type=F32)


def _silu(t):
    return t * jax.nn.sigmoid(t)


def _mod_kernel(lam_init, cc_ref, w_ref, b_ref, lamp_ref, mod_ref, lam_ref):
    cc = cc_ref[...]
    mod_ref[...] = _dot_exact(_silu(cc), w_ref[...]) + b_ref[...]
    lp = lamp_ref[...]
    e1 = jnp.exp(jnp.sum(lp[0:1] * lp[1:2], axis=-1, keepdims=True))
    e2 = jnp.exp(jnp.sum(lp[2:3] * lp[3:4], axis=-1, keepdims=True))
    lam_ref[...] = jnp.broadcast_to(e1 - e2 + lam_init, lam_ref.shape)


def _mod_call(cc, w_ada, b_ada, lam_params, lam_init):
    d = cc.shape[1]
    n = w_ada.shape[1]
    return pl.pallas_call(
        functools.partial(_mod_kernel, lam_init),
        grid=(n // d,),
        in_specs=[
            pl.BlockSpec((8, d), lambda j: (0, 0)),
            pl.BlockSpec((d, d), lambda j: (0, j)),
            pl.BlockSpec((1, d), lambda j: (0, j)),
            pl.BlockSpec(lam_params.shape, lambda j: (0, 0)),
        ],
        out_specs=[
            pl.BlockSpec((8, d), lambda j: (0, j)),
            pl.BlockSpec((8, LANES), lambda j: (0, 0)),
        ],
        out_shape=[jax.ShapeDtypeStruct((8, n), F32), jax.ShapeDtypeStruct((8, LANES), F32)],
        compiler_params=pltpu.CompilerParams(dimension_semantics=("arbitrary",)),
        name="mod",
    )(cc, w_ada, b_ada, lam_params)


def _group_norm_rope(p, gsum_ref, gain_ref, cos, sin_signed, lo_mask):
    tm, d = p.shape
    sq = (p * p).astype(BF16)
    slabs = []
    for s in range(d // 256):
        slabs.append(_dot(sq[:, s * 256:(s + 1) * 256], gsum_ref[...]))
    ss = jnp.concatenate(slabs, axis=1)
    n = p * lax.rsqrt(ss * (1.0 / DIFF_DK) + EPS) * gain_ref[...]
    outs = []
    for h in range(d // LANES):
        sl = n[:, h * LANES:(h + 1) * LANES]
        up = pltpu.roll(sl, LANES - ROPE_AXIS_DIM // 2, axis=1)
        dn = pltpu.roll(sl, ROPE_AXIS_DIM // 2, axis=1)
        outs.append(sl * cos + jnp.where(lo_mask, up, dn) * sin_signed)
    return jnp.concatenate(outs, axis=1)


def _proj_kernel(offs, x_ref, ctx_ref, shift_ref, scale_ref, w_ref, wdec_ref, bdec_ref,
                 qgain_ref, kgain_ref, cos_ref, sin_ref, trif_ref, trib_ref, gsum_ref,
                 gq_ref, gk_ref, gv_ref, cf_ref, cb_ref, gg_ref, dg_ref, mg_ref, md_ref,
                 dq_ref, dk_ref, dvt_ref):
    t = pl.program_id(1)
    latent = t > 0
    xin = jnp.where(latent, x_ref[0], ctx_ref[0])
    ms = jnp.mean(xin * xin, axis=-1, keepdims=True)
    h = xin * lax.rsqrt(ms + EPS) * (1.0 + scale_ref[0]) + shift_ref[0]
    hb = h.astype(BF16)

    def proj(name):
        a, b = offs[name]
        return _dot(hb, w_ref[:, a:b])

    dk_gla = gq_ref.shape[2] // GLA_HEADS
    gq_ref[0] = proj("gq") * dk_gla ** -0.5
    gk_ref[0] = proj("gk")
    gv_ref[0] = proj("gv").astype(BF16)

    z = _dot_exact(proj("lr"), wdec_ref[...]) + bdec_ref[...]
    la = (jnp.minimum(z, 0.0) - jnp.log1p(jnp.exp(-jnp.abs(z)))) * (1.0 / GLA_TAU)
    half = la.shape[1] // 2
    cf_ref[0] = _dot_exact(trif_ref[...], la[:, :half])
    cb_ref[0] = _dot_exact(trib_ref[...], la[:, half:])

    gg_ref[0] = proj("gg").astype(BF16)
    dg_ref[0] = proj("dg").astype(BF16)
    mg_ref[0] = proj("mg").astype(BF16)
    md_ref[0] = proj("md").astype(BF16)

    cos = cos_ref[...]
    sin_signed = sin_ref[...]
    lane = lax.broadcasted_iota(jnp.int32, cos.shape, 1)
    lo_mask = (lane % ROPE_AXIS_DIM) < (ROPE_AXIS_DIM // 2)
    dq_ref[0] = _group_norm_rope(proj("dq"), gsum_ref, qgain_ref, cos, sin_signed, lo_mask).astype(BF16)
    dk_ref[0] = _group_norm_rope(proj("dk"), gsum_ref, kgain_ref, cos, sin_signed, lo_mask).astype(BF16)
    dvt_ref[0] = proj("dv").T.astype(BF16)


def _proj_call(x, ctx, mod3, w_p, offs, wdec, bdec, qgain, kgain, cos, sin_signed, trif, trib, gsum):
    b_, l_, d = x.shape
    tm = ROW_TILE
    assert ctx.shape[1] == tm and l_ % tm == 0
    nt = l_ // tm + 1
    lt = l_ + tm
    n_ctx_row = b_
    half_k = offs["gq"][1] - offs["gq"][0]

    lat = lambda b, t: (b, jnp.maximum(t - 1, 0), 0)
    allr = lambda b, t: (b, t, 0)
    const2 = lambda b, t: (0, 0)
    once = pl.Buffered(1)

    def mod_spec(col):
        return pl.BlockSpec((1, 1, d), lambda b, t: (jnp.where(t == 0, n_ctx_row, b), 0, col))

    in_specs = [
        pl.BlockSpec((1, tm, d), lat),
        pl.BlockSpec((1, tm, d), lambda b, t: (b, 0, 0)),
        mod_spec(0), mod_spec(1),
        pl.BlockSpec(w_p.shape, const2, pipeline_mode=once),
        pl.BlockSpec(wdec.shape, const2, pipeline_mode=once),
        pl.BlockSpec(bdec.shape, const2, pipeline_mode=once),
        pl.BlockSpec(qgain.shape, const2, pipeline_mode=once),
        pl.BlockSpec(kgain.shape, const2, pipeline_mode=once),
        pl.BlockSpec((tm, LANES), lambda b, t: (t, 0)),
        pl.BlockSpec((tm, LANES), lambda b, t: (t, 0)),
        pl.BlockSpec(trif.shape, const2, pipeline_mode=once),
        pl.BlockSpec(trib.shape, const2, pipeline_mode=once),
        pl.BlockSpec(gsum.shape, const2, pipeline_mode=once),
    ]
    out_specs = [
        pl.BlockSpec((1, tm, half_k), lat),
        pl.BlockSpec((1, tm, half_k), allr),
        pl.BlockSpec((1, tm, d), allr),
        pl.BlockSpec((1, tm, half_k), allr),
        pl.BlockSpec((1, tm, half_k), allr),
        pl.BlockSpec((1, tm, d), lat),
        pl.BlockSpec((1, tm, d), lat),
        pl.BlockSpec((1, tm, d), lat),
        pl.BlockSpec((1, tm, d), lat),
        pl.BlockSpec((1, tm, d), lat),
        pl.BlockSpec((1, tm, d), allr),
        pl.BlockSpec((1, d, tm), lambda b, t: (b, 0, t)),
    ]
    sd = jax.ShapeDtypeStruct
    out_shape = [
        sd((b_, l_, half_k), F32), sd((b_, lt, half_k), F32), sd((b_, lt, d), BF16),
        sd((b_, lt, half_k), F32), sd((b_, lt, half_k), F32),
        sd((b_, l_, d), BF16), sd((b_, l_, d), BF16), sd((b_, l_, d), BF16), sd((b_, l_, d), BF16),
        sd((b_, l_, d), BF16), sd((b_, lt, d), BF16), sd((b_, d, lt), BF16),
    ]
    return pl.pallas_call(
        functools.partial(_proj_kernel, offs),
        grid=(b_, nt),
        in_specs=in_specs, out_specs=out_specs, out_shape=out_shape,
        compiler_params=pltpu.CompilerParams(dimension_semantics=("arbitrary", "arbitrary"),
                                             vmem_limit_bytes=VMEM_LIMIT),
        name="proj",
    )(x, ctx, mod3, mod3, w_p, wdec, bdec, qgain, kgain, cos, sin_signed, trif, trib, gsum)


def _gla_direction(forward, q_ref, k_ref, v_ref, c_ref, o_ref, s_ref):
    tm = k_ref.shape[1]
    nh = GLA_HEADS
    dk = k_ref.shape[2] // nh
    dv = v_ref.shape[2] // nh
    nc = tm // GLA_CHUNK
    ri = lax.broadcasted_iota(jnp.int32, (tm, tm), 0)
    ci = lax.broadcasted_iota(jnp.int32, (tm, tm), 1)
    same = (ri // GLA_CHUNK) == (ci // GLA_CHUNK)
    mask = same & ((ri >= ci) if forward else (ci >= ri))
    lane_chunk = lax.broadcasted_iota(jnp.int32, (dk, tm), 1) // GLA_CHUNK
    half = GLA_CHUNK // 2
    for h in range(nh):
        ks = slice(h * dk, (h + 1) * dk)
        vs = slice(h * dv, (h + 1) * dv)
        cum = c_ref[0, :, ks]
        q = q_ref[0, :, ks]
        k = k_ref[0, :, ks]
        v = v_ref[0, :, vs]
        last_rows, ref_rows = [], []
        for c in range(nc):
            base = c * GLA_CHUNK
            li = base + GLA_CHUNK - 1 if forward else base
            ce = base + half - 1 if forward else base + half
            last_rows.append(jnp.broadcast_to(cum[li:li + 1], (GLA_CHUNK, dk)))
            ref_rows.append(jnp.broadcast_to(cum[ce:ce + 1], (GLA_CHUNK, dk)))
        lastb = jnp.concatenate(last_rows, axis=0)
        refb = jnp.concatenate(ref_rows, axis=0)
        qe = (q * jnp.exp(cum - refb)).astype(BF16)
        ke = (k * jnp.exp(refb - cum)).astype(BF16)
        a = jnp.where(mask, _dot_nt(qe, ke), 0.0).astype(BF16)
        o_intra = _dot(a, v)
        qd = (q * jnp.exp(cum)).astype(BF16)
        kdt = (k * jnp.exp(lastb - cum)).T.astype(BF16)
        dect = jnp.exp(lastb).T
        state = s_ref[h]
        parts = [None] * nc
        for c in (range(nc) if forward else reversed(range(nc))):
            rows = slice(c * GLA_CHUNK, (c + 1) * GLA_CHUNK)
            parts[c] = _dot(qd[rows], state.astype(BF16))
            upd = _dot(jnp.where(lane_chunk == c, kdt, jnp.zeros_like(kdt)), v)
            dcol = jnp.broadcast_to(dect[:, c * GLA_CHUNK:c * GLA_CHUNK + 1], (dk, dv))
            state = dcol * state + upd
        s_ref[h] = state
        o_ref[0, :, vs] = (o_intra + jnp.concatenate(parts, axis=0)).astype(o_ref.dtype)


def _gla_kernel(qf_ref, kf_ref, vf_ref, cf_ref, qb_ref, kb_ref, vb_ref, cb_ref,
                of_ref, ob_ref, sf_ref, sb_ref):
    @pl.when(pl.program_id(1) == 0)
    def _():
        sf_ref[...] = jnp.zeros_like(sf_ref)
        sb_ref[...] = jnp.zeros_like(sb_ref)

    _gla_direction(True, qf_ref, kf_ref, vf_ref, cf_ref, of_ref, sf_ref)
    _gla_direction(False, qb_ref, kb_ref, vb_ref, cb_ref, ob_ref, sb_ref)


def _gla_call(gq, gk, gv, cumf, cumb):
    b_, l_, kw = gq.shape
    lt = gk.shape[1]
    d = gv.shape[2]
    tm = ROW_TILE
    nt = lt // tm
    nlat = l_ // tm
    dk = kw // GLA_HEADS
    dv = d // GLA_HEADS

    f_lat = lambda b, s: (b, jnp.maximum(s - 1, 0), 0)
    f_all = lambda b, s: (b, s, 0)
    b_lat = lambda b, s: (b, jnp.where(s == 0, nlat - 1, nlat - s), 0)
    b_all = lambda b, s: (b, jnp.where(s == 0, 0, nt - s), 0)
    return pl.pallas_call(
        _gla_kernel,
        grid=(b_, nt),
        in_specs=[
            pl.BlockSpec((1, tm, kw), f_lat), pl.BlockSpec((1, tm, kw), f_all),
            pl.BlockSpec((1, tm, d), f_all), pl.BlockSpec((1, tm, kw), f_all),
            pl.BlockSpec((1, tm, kw), b_lat), pl.BlockSpec((1, tm, kw), b_all),
            pl.BlockSpec((1, tm, d), b_all), pl.BlockSpec((1, tm, kw), b_all),
        ],
        out_specs=[pl.BlockSpec((1, tm, d), f_lat), pl.BlockSpec((1, tm, d), b_lat)],
        out_shape=[jax.ShapeDtypeStruct((b_, l_, d), BF16), jax.ShapeDtypeStruct((b_, l_, d), BF16)],
        scratch_shapes=[pltpu.VMEM((GLA_HEADS, dk, dv), F32), pltpu.VMEM((GLA_HEADS, dk, dv), F32)],
        compiler_params=pltpu.CompilerParams(dimension_semantics=("arbitrary", "arbitrary"),
                                             vmem_limit_bytes=VMEM_LIMIT),
        name="gla",
    )(gq, gk, gv, cumf, gq, gk, gv, cumb)


def _attn_kernel(tk, lam_ref, q_ref, k_ref, vt_ref, o_ref, acc1, acc2, m1, l1, m2, l2):
    tq = q_ref.shape[1]
    nk = k_ref.shape[1] // tk
    q = q_ref[0]
    lane = lax.broadcasted_iota(jnp.int32, q.shape, 1)
    zero = jnp.zeros_like(q)
    q1 = jnp.where(lane < DIFF_DK, q, zero)
    q2 = jnp.where(lane >= DIFF_DK, q, zero)
    for r in (acc1, acc2, l1, l2):
        r[...] = jnp.zeros_like(r)
    m1[...] = jnp.full_like(m1, -1e30)
    m2[...] = jnp.full_like(m2, -1e30)

    def body(j, carry):
        off = pl.multiple_of(j * tk, tk)
        kk = k_ref[0, pl.ds(off, tk), :]
        vt = vt_ref[0, :, pl.ds(off, tk)]
        for qq, acc, m, l in ((q1, acc1, m1, l1), (q2, acc2, m2, l2)):
            s = _dot_nt(kk, qq)
            m_new = jnp.maximum(m[...], jnp.max(s, axis=0, keepdims=True))
            alpha = jnp.exp(m[...] - m_new)
            p = jnp.exp(s - m_new)
            l[...] = alpha * l[...] + jnp.sum(p, axis=0, keepdims=True)
            acc[...] = alpha * acc[...] + _dot(vt, p.astype(BF16))
            m[...] = m_new
        return carry

    lax.fori_loop(0, nk, body, 0)
    o = acc1[...] * (1.0 / l1[...]) - lam_ref[0] * (acc2[...] * (1.0 / l2[...]))
    o = o * lax.rsqrt(jnp.mean(o * o, axis=0, keepdims=True) + EPS)
    o_ref[0] = o.T.astype(o_ref.dtype)


def _attn_call(lam, dq, dk, dvt, tq, tk):
    b_, l_, d = dq.shape
    lt = dk.shape[1]
    nh = d // LANES
    assert l_ % tq == 0 and lt % tk == 0
    return pl.pallas_call(
        functools.partial(_attn_kernel, tk),
        grid=(b_, nh, l_ // tq),
        in_specs=[
            pl.BlockSpec(memory_space=pltpu.SMEM),
            pl.BlockSpec((1, tq, LANES), lambda b, h, i: (b, i, h)),
            pl.BlockSpec((1, lt, LANES), lambda b, h, i: (b, 0, h)),
            pl.BlockSpec((1, LANES, lt), lambda b, h, i: (b, h, 0)),
        ],
        out_specs=pl.BlockSpec((1, tq, LANES), lambda b, h, i: (b, i, h)),
        out_shape=jax.ShapeDtypeStruct((b_, l_, d), BF16),
        scratch_shapes=[pltpu.VMEM((LANES, tq), F32), pltpu.VMEM((LANES, tq), F32),
                        pltpu.VMEM((1, tq), F32), pltpu.VMEM((1, tq), F32),
                        pltpu.VMEM((1, tq), F32), pltpu.VMEM((1, tq), F32)],
        compiler_params=pltpu.CompilerParams(
            dimension_semantics=("arbitrary", "arbitrary", "arbitrary"),
            vmem_limit_bytes=VMEM_LIMIT),
        name="diffattn",
    )(lam, dq, dk, dvt)


def _combine_kernel(x_ref, g_ref, of_ref, ob_ref, od_ref, gg_ref, dg_ref, mg_ref, md_ref,
                    ggain_ref, dgain_ref, wg_ref, wd_ref, wo_ref, out_ref):
    d = x_ref.shape[2]
    dv = d // GLA_HEADS
    og = of_ref[0].astype(F32) + ob_ref[0].astype(F32)
    normed = []
    for h in range(GLA_HEADS):
        sl = og[:, h * dv:(h + 1) * dv]
        normed.append(sl * lax.rsqrt(jnp.mean(sl * sl, axis=-1, keepdims=True) + EPS))
    a = jnp.concatenate(normed, axis=1) * ggain_ref[...] * _silu(gg_ref[0].astype(F32))
    bb = od_ref[0].astype(F32) * dgain_ref[...] * _silu(dg_ref[0].astype(F32))
    y = (jax.nn.sigmoid(mg_ref[0].astype(F32)) * _dot(a.astype(BF16), wg_ref[...])
         + jax.nn.sigmoid(md_ref[0].astype(F32)) * _dot(bb.astype(BF16), wd_ref[...]))
    out_ref[0] = x_ref[0] + g_ref[0] * _dot(y.astype(BF16), wo_ref[...])


def _combine_call(x, mod3, o_f, o_b, o_d, gg, dg, mg, md, ggain, dgain, wg, wd, wo, tm):
    b_, l_, d = x.shape
    row = lambda b, i: (b, i, 0)
    const2 = lambda b, i: (0, 0)
    once = pl.Buffered(1)
    big = pl.BlockSpec((1, tm, d), row)
    wspec = pl.BlockSpec((d, d), const2, pipeline_mode=once)
    return pl.pallas_call(
        _combine_kernel,
        grid=(b_, l_ // tm),
        in_specs=[big, pl.BlockSpec((1, 1, d), lambda b, i: (b, 0, 2)),
                  big, big, big, big, big, big, big,
                  pl.BlockSpec((1, d), const2, pipeline_mode=once),
                  pl.BlockSpec((1, d), const2, pipeline_mode=once),
                  wspec, wspec, wspec],
        out_specs=big,
        out_shape=jax.ShapeDtypeStruct(x.shape, x.dtype),
        compiler_params=pltpu.CompilerParams(dimension_semantics=("arbitrary", "arbitrary"),
                                             vmem_limit_bytes=VMEM_LIMIT),
        name="combine",
    )(x, mod3, o_f, o_b, o_d, gg, dg, mg, md, ggain, dgain, wg, wd, wo)


def _rope_tables(n_tokens, n_ctx):
    rows = n_tokens // GRID_W
    row = jnp.repeat(jnp.arange(rows, dtype=F32), GRID_W)
    col = jnp.tile(jnp.arange(GRID_W, dtype=F32), rows)
    inv_freq = ROPE_BASE ** (-jnp.arange(0, ROPE_AXIS_DIM, 2, dtype=F32) / ROPE_AXIS_DIM)
    ang_r = row[:, None] * inv_freq
    ang_c = col[:, None] * inv_freq
    two = lambda t: jnp.concatenate([t, t], axis=-1)
    cos = jnp.concatenate([two(jnp.cos(ang_r)), two(jnp.cos(ang_c))] * 2, axis=-1)
    sin = jnp.concatenate([two(jnp.sin(ang_r)), two(jnp.sin(ang_c))] * 2, axis=-1)
    sign = jnp.where((jnp.arange(LANES) % ROPE_AXIS_DIM) < ROPE_AXIS_DIM // 2, -1.0, 1.0).astype(F32)
    cos = jnp.concatenate([jnp.ones((n_ctx, LANES), F32), cos], axis=0)
    sin = jnp.concatenate([jnp.zeros((n_ctx, LANES), F32), sin * sign], axis=0)
    return cos, sin


def _block_tri(n, upper):
    i = np.arange(n)
    same = (i[:, None] // GLA_CHUNK) == (i[None, :] // GLA_CHUNK)
    tri = (i[None, :] >= i[:, None]) if upper else (i[:, None] >= i[None, :])
    return jnp.asarray((same & tri).astype(np.float32))


def _attn_tiles(l_, lt):
    tq = 512 if l_ % 512 == 0 else l_
    for tk in (768, 512, 384, 256, 128):
        if lt % tk == 0:
            return tq, tk
    raise ValueError("key length must be a multiple of 128")


def kernel(x, c, ctx, c_ctx, w_ada, b_ada, w_in, gla_w_decay, gla_b_decay, gla_norm, diff_q_norm,
           diff_k_norm, diff_lambda, diff_norm, w_br_gla, w_br_diff, w_out):
    b_, l_, d = x.shape
    n_ctx = ctx.shape[1]
    depth = w_ada.shape[0]
    assert depth == 1, "single-layer block"
    layer = 0
    lam_init = 0.8 - 0.6 * math.exp(-0.3 * layer)
    half_k = d // 2

    cc = jnp.zeros((8, d), F32).at[:b_].set(c).at[b_].set(c_ctx)
    mod, lam_tile = _mod_call(cc, w_ada[layer], b_ada[layer][None, :], diff_lambda[layer], lam_init)
    mod3 = mod.reshape(8, 1, 3 * d)
    lam = lam_tile[0, :1]

    w = w_in[layer]
    sizes = (half_k, half_k, d, d, 2 * GLA_RANK, d, d, d, d, d, d)
    starts = np.concatenate([[0], np.cumsum(sizes)])
    col = lambda i: w[:, int(starts[i]):int(starts[i + 1])]
    names = ("gq", "gk", "gv", "gg", "dq", "dk", "dv", "dg", "mg", "md", "lr")
    order = (0, 1, 2, 3, 5, 6, 7, 8, 9, 10)
    pieces = [col(i) for i in order] + [col(4), jnp.zeros((d, LANES - 2 * GLA_RANK), w.dtype)]
    w_p = jnp.concatenate(pieces, axis=1).astype(BF16)
    widths = [sizes[i] for i in order] + [LANES]
    offs, pos = {}, 0
    for name, wd_ in zip(names, widths):
        offs[name] = (pos, pos + wd_)
        pos += wd_

    wdec = jnp.zeros((LANES, 2 * half_k), F32)
    wdec = wdec.at[:GLA_RANK, :half_k].set(gla_w_decay[layer, 0])
    wdec = wdec.at[GLA_RANK:2 * GLA_RANK, half_k:].set(gla_w_decay[layer, 1])
    bdec = jnp.concatenate([gla_b_decay[layer, 0], gla_b_decay[layer, 1]])[None, :]
    reps = d // DIFF_DK
    qgain = (jnp.tile(diff_q_norm[layer], reps) * DIFF_DK ** -0.5)[None, :]
    kgain = jnp.tile(diff_k_norm[layer], reps)[None, :]
    cos, sin_signed = _rope_tables(l_, n_ctx)
    trif = _block_tri(ROW_TILE, upper=False)
    trib = _block_tri(ROW_TILE, upper=True)
    gi = np.arange(256)
    gsum = jnp.asarray((gi[:, None] // DIFF_DK == gi[None, :] // DIFF_DK).astype(np.float32)).astype(BF16)

    (gq, gk, gv, cumf, cumb, gg, dg, mg, md, dq, dk, dvt) = _proj_call(
        x, ctx, mod3, w_p, offs, wdec, bdec, qgain, kgain, cos, sin_signed, trif, trib, gsum)

    o_f, o_b = _gla_call(gq, gk, gv, cumf, cumb)

    tq, tk = _attn_tiles(l_, l_ + n_ctx)
    o_d = _attn_call(lam, dq, dk, dvt, tq, tk)

    ggain = jnp.tile(gla_norm[layer], GLA_HEADS)[None, :]
    dgain = (jnp.tile(diff_norm[layer], d // diff_norm.shape[1]) * (1.0 - lam_init))[None, :]
    tm = 512 if l_ % 512 == 0 else l_
    return _combine_call(x, mod3, o_f, o_b, o_d, gg, dg, mg, md, ggain, dgain,
                         w_br_gla[layer].astype(BF16), w_br_diff[layer].astype(BF16),
                         w_out[layer].astype(BF16), tm)
```

```python
import functools
import math

import jax
import jax.numpy as jnp
import numpy as np
from jax import lax
from jax.experimental import pallas as pl
from jax.experimental.pallas import tpu as pltpu

F32 = jnp.float32
BF16 = jnp.bfloat16
HIGHEST = lax.Precision.HIGHEST

EPS = 1e-6
GRID_W = 64
GLA_HEADS = 4
GLA_RANK = 16
GLA_TAU = 16.0
GLA_CHUNK = 64
DIFF_DK = 64
ROPE_BASE = 10000.0
ROPE_AXIS_DIM = DIFF_DK // 2
LANES = 128
ROW_TILE = 256
VMEM_LIMIT = 56 * 1024 * 1024
LOG2E = math.log2(math.e)
SCORE_BOUND = 40.0
BOUND_MARGIN = 1.02


def _dot(a, b):
    return jnp.dot(a, b, preferred_element_type=F32)


def _dot_nt(a, b):
    return lax.dot_general(a, b, (((1,), (1,)), ((), ())), preferred_element_type=F32)


def _dot_exact(a, b):
    return lax.dot_general(a, b, (((1,), (0,)), ((), ())), precision=HIGHEST,
                           preferred_element_type=F32)


def _split_bf16(a):
    hi = a.astype(BF16)
    return hi, (a - hi.astype(F32)).astype(BF16)


def _silu(t):
    return t * jax.nn.sigmoid(t)


def _mod_kernel(lam_init, cc_ref, w_ref, b_ref, lamp_ref, mod_ref, lam_ref):
    cc = cc_ref[...]
    mod_ref[...] = _dot_exact(_silu(cc), w_ref[...]) + b_ref[...]
    lp = lamp_ref[...]
    e1 = jnp.exp(jnp.sum(lp[0:1] * lp[1:2], axis=-1, keepdims=True))
    e2 = jnp.exp(jnp.sum(lp[2:3] * lp[3:4], axis=-1, keepdims=True))
    lam_ref[...] = jnp.broadcast_to(e1 - e2 + lam_init, lam_ref.shape)


def _mod_call(cc, w_ada, b_ada, lam_params, lam_init):
    d = cc.shape[1]
    n = w_ada.shape[1]
    return pl.pallas_call(
        functools.partial(_mod_kernel, lam_init),
        grid=(n // d,),
        in_specs=[
            pl.BlockSpec((8, d), lambda j: (0, 0)),
            pl.BlockSpec((d, d), lambda j: (0, j)),
            pl.BlockSpec((1, d), lambda j: (0, j)),
            pl.BlockSpec(lam_params.shape, lambda j: (0, 0)),
        ],
        out_specs=[
            pl.BlockSpec((8, d), lambda j: (0, j)),
            pl.BlockSpec((8, LANES), lambda j: (0, 0)),
        ],
        out_shape=[jax.ShapeDtypeStruct((8, n), F32), jax.ShapeDtypeStruct((8, LANES), F32)],
        compiler_params=pltpu.CompilerParams(dimension_semantics=("arbitrary",)),
        name="mod",
    )(cc, w_ada, b_ada, lam_params)


def _group_norm_rope(p, gsum_ref, gain_ref, cos, sin_signed, lo_mask):
    tm, d = p.shape
    sq = (p * p).astype(BF16)
    slabs = []
    for s in range(d // 256):
        slabs.append(_dot(sq[:, s * 256:(s + 1) * 256], gsum_ref[...]))
    ss = jnp.concatenate(slabs, axis=1)
    n = p * lax.rsqrt(ss * (1.0 / DIFF_DK) + EPS) * gain_ref[...]
    outs = []
    for h in range(d // LANES):
        sl = n[:, h * LANES:(h + 1) * LANES]
        up = pltpu.roll(sl, LANES - ROPE_AXIS_DIM // 2, axis=1)
        dn = pltpu.roll(sl, ROPE_AXIS_DIM // 2, axis=1)
        outs.append(sl * cos + jnp.where(lo_mask, up, dn) * sin_signed)
    return jnp.concatenate(outs, axis=1)


def _proj_kernel(offs, x_ref, ctx_ref, shift_ref, scale_ref, w_ref, wdec_ref, bdec_ref,
                 qgain_ref, kgain_ref, cos_ref, sin_ref, trif_ref, trib_ref, gsum_ref,
                 gq_ref, gk_ref, gv_ref, cf_ref, cb_ref, gg_ref, dg_ref, mg_ref, md_ref,
                 dq_ref, dk_ref, dvt_ref):
    t = pl.program_id(1)
    latent = t > 0
    xin = jnp.where(latent, x_ref[0], ctx_ref[0])
    ms = jnp.mean(xin * xin, axis=-1, keepdims=True)
    h = xin * lax.rsqrt(ms + EPS) * (1.0 + scale_ref[0]) + shift_ref[0]
    hb = h.astype(BF16)

    def proj(name):
        a, b = offs[name]
        return _dot(hb, w_ref[:, a:b])

    dk_gla = gq_ref.shape[2] // GLA_HEADS
    gq_ref[0] = proj("gq") * dk_gla ** -0.5
    gk_ref[0] = proj("gk")
    gv_ref[0] = proj("gv").astype(BF16)

    lr3 = proj("lr")
    lr_hi, lr_lo = _split_bf16(lr3)
    lane = lax.broadcasted_iota(jnp.int32, lr3.shape, 1)
    z = _dot(jnp.where(lane < 4 * GLA_RANK, lr_hi, lr_lo), wdec_ref[...]) + bdec_ref[...]
    la = (jnp.minimum(z, 0.0) - jnp.log1p(jnp.exp(-jnp.abs(z)))) * (1.0 / GLA_TAU)
    half = la.shape[1] // 2
    la_hi, la_lo = _split_bf16(la)
    cf_ref[0] = _dot(trif_ref[...], la_hi[:, :half]) + _dot(trif_ref[...], la_lo[:, :half])
    cb_ref[0] = _dot(trib_ref[...], la_hi[:, half:]) + _dot(trib_ref[...], la_lo[:, half:])

    gg_ref[0] = proj("gg").astype(BF16)
    dg_ref[0] = proj("dg").astype(BF16)
    mg_ref[0] = proj("mg").astype(BF16)
    md_ref[0] = proj("md").astype(BF16)

    cos = cos_ref[...]
    sin_signed = sin_ref[...]
    lane = lax.broadcasted_iota(jnp.int32, cos.shape, 1)
    lo_mask = (lane % ROPE_AXIS_DIM) < (ROPE_AXIS_DIM // 2)
    dq_ref[0] = _group_norm_rope(proj("dq"), gsum_ref, qgain_ref, cos, sin_signed, lo_mask).astype(BF16)
    dk_ref[0] = _group_norm_rope(proj("dk"), gsum_ref, kgain_ref, cos, sin_signed, lo_mask).astype(BF16)
    dvt_ref[0] = proj("dv").T.astype(BF16)


def _proj_call(x, ctx, mod3, w_p, offs, wdec, bdec, qgain, kgain, cos, sin_signed, trif, trib, gsum):
    b_, l_, d = x.shape
    tm = ROW_TILE
    assert ctx.shape[1] == tm and l_ % tm == 0
    nt = l_ // tm + 1
    lt = l_ + tm
    n_ctx_row = b_
    half_k = offs["gq"][1] - offs["gq"][0]

    lat = lambda b, t: (b, jnp.maximum(t - 1, 0), 0)
    allr = lambda b, t: (b, t, 0)
    const2 = lambda b, t: (0, 0)
    once = pl.Buffered(1)

    def mod_spec(col):
        return pl.BlockSpec((1, 1, d), lambda b, t: (jnp.where(t == 0, n_ctx_row, b), 0, col))

    in_specs = [
        pl.BlockSpec((1, tm, d), lat),
        pl.BlockSpec((1, tm, d), lambda b, t: (b, 0, 0)),
        mod_spec(0), mod_spec(1),
        pl.BlockSpec(w_p.shape, const2, pipeline_mode=once),
        pl.BlockSpec(wdec.shape, const2, pipeline_mode=once),
        pl.BlockSpec(bdec.shape, const2, pipeline_mode=once),
        pl.BlockSpec(qgain.shape, const2, pipeline_mode=once),
        pl.BlockSpec(kgain.shape, const2, pipeline_mode=once),
        pl.BlockSpec((tm, LANES), lambda b, t: (t, 0)),
        pl.BlockSpec((tm, LANES), lambda b, t: (t, 0)),
        pl.BlockSpec(trif.shape, const2, pipeline_mode=once),
        pl.BlockSpec(trib.shape, const2, pipeline_mode=once),
        pl.BlockSpec(gsum.shape, const2, pipeline_mode=once),
    ]
    out_specs = [
        pl.BlockSpec((1, tm, half_k), lat),
        pl.BlockSpec((1, tm, half_k), allr),
        pl.BlockSpec((1, tm, d), allr),
        pl.BlockSpec((1, tm, half_k), allr),
        pl.BlockSpec((1, tm, half_k), allr),
        pl.BlockSpec((1, tm, d), lat),
        pl.BlockSpec((1, tm, d), lat),
        pl.BlockSpec((1, tm, d), lat),
        pl.BlockSpec((1, tm, d), lat),
        pl.BlockSpec((1, tm, d), lat),
        pl.BlockSpec((1, tm, d), allr),
        pl.BlockSpec((1, d, tm), lambda b, t: (b, 0, t)),
    ]
    sd = jax.ShapeDtypeStruct
    out_shape = [
        sd((b_, l_, half_k), F32), sd((b_, lt, half_k), F32), sd((b_, lt, d), BF16),
        sd((b_, lt, half_k), F32), sd((b_, lt, half_k), F32),
        sd((b_, l_, d), BF16), sd((b_, l_, d), BF16), sd((b_, l_, d), BF16), sd((b_, l_, d), BF16),
        sd((b_, l_, d), BF16), sd((b_, lt, d), BF16), sd((b_, d, lt), BF16),
    ]
    return pl.pallas_call(
        functools.partial(_proj_kernel, offs),
        grid=(b_, nt),
        in_specs=in_specs, out_specs=out_specs, out_shape=out_shape,
        compiler_params=pltpu.CompilerParams(dimension_semantics=("arbitrary", "arbitrary"),
                                             vmem_limit_bytes=VMEM_LIMIT),
        name="proj",
    )(x, ctx, mod3, mod3, w_p, wdec, bdec, qgain, kgain, cos, sin_signed, trif, trib, gsum)


def _gla_direction(forward, q_ref, k_ref, v_ref, c_ref, o_ref, s_ref):
    tm = k_ref.shape[1]
    nh = GLA_HEADS
    dk = k_ref.shape[2] // nh
    dv = v_ref.shape[2] // nh
    nc = tm // GLA_CHUNK
    ri = lax.broadcasted_iota(jnp.int32, (tm, tm), 0)
    ci = lax.broadcasted_iota(jnp.int32, (tm, tm), 1)
    same = (ri // GLA_CHUNK) == (ci // GLA_CHUNK)
    mask = same & ((ri >= ci) if forward else (ci >= ri))
    lane_chunk = lax.broadcasted_iota(jnp.int32, (dk, tm), 1) // GLA_CHUNK
    half = GLA_CHUNK // 2
    for h in range(nh):
        ks = slice(h * dk, (h + 1) * dk)
        vs = slice(h * dv, (h + 1) * dv)
        cum = c_ref[0, :, ks]
        q = q_ref[0, :, ks]
        k = k_ref[0, :, ks]
        v = v_ref[0, :, vs]
        last_rows, ref_rows = [], []
        for c in range(nc):
            base = c * GLA_CHUNK
            li = base + GLA_CHUNK - 1 if forward else base
            ce = base + half - 1 if forward else base + half
            last_rows.append(jnp.broadcast_to(cum[li:li + 1], (GLA_CHUNK, dk)))
            ref_rows.append(jnp.broadcast_to(cum[ce:ce + 1], (GLA_CHUNK, dk)))
        lastb = jnp.concatenate(last_rows, axis=0)
        refb = jnp.concatenate(ref_rows, axis=0)
        qe = (q * jnp.exp(cum - refb)).astype(BF16)
        ke = (k * jnp.exp(refb - cum)).astype(BF16)
        a = jnp.where(mask, _dot_nt(qe, ke), 0.0).astype(BF16)
        o_intra = _dot(a, v)
        qd = (q * jnp.exp(cum)).astype(BF16)
        kdt = (k * jnp.exp(lastb - cum)).T.astype(BF16)
        dect = jnp.exp(lastb).T
        state = s_ref[h]
        parts = [None] * nc
        for c in (range(nc) if forward else reversed(range(nc))):
            rows = slice(c * GLA_CHUNK, (c + 1) * GLA_CHUNK)
            parts[c] = _dot(qd[rows], state.astype(BF16))
            upd = _dot(jnp.where(lane_chunk == c, kdt, jnp.zeros_like(kdt)), v)
            dcol = jnp.broadcast_to(dect[:, c * GLA_CHUNK:c * GLA_CHUNK + 1], (dk, dv))
            state = dcol * state + upd
        s_ref[h] = state
        o_ref[0, :, vs] = (o_intra + jnp.concatenate(parts, axis=0)).astype(o_ref.dtype)


def _gla_kernel(qf_ref, kf_ref, vf_ref, cf_ref, qb_ref, kb_ref, vb_ref, cb_ref,
                of_ref, ob_ref, sf_ref, sb_ref):
    @pl.when(pl.program_id(1) == 0)
    def _():
        sf_ref[...] = jnp.zeros_like(sf_ref)
        sb_ref[...] = jnp.zeros_like(sb_ref)

    _gla_direction(True, qf_ref, kf_ref, vf_ref, cf_ref, of_ref, sf_ref)
    _gla_direction(False, qb_ref, kb_ref, vb_ref, cb_ref, ob_ref, sb_ref)


def _gla_call(gq, gk, gv, cumf, cumb):
    b_, l_, kw = gq.shape
    lt = gk.shape[1]
    d = gv.shape[2]
    tm = ROW_TILE
    nt = lt // tm
    nlat = l_ // tm
    dk = kw // GLA_HEADS
    dv = d // GLA_HEADS

    f_lat = lambda b, s: (b, jnp.maximum(s - 1, 0), 0)
    f_all = lambda b, s: (b, s, 0)
    b_lat = lambda b, s: (b, jnp.where(s == 0, nlat - 1, nlat - s), 0)
    b_all = lambda b, s: (b, jnp.where(s == 0, 0, nt - s), 0)
    return pl.pallas_call(
        _gla_kernel,
        grid=(b_, nt),
        in_specs=[
            pl.BlockSpec((1, tm, kw), f_lat), pl.BlockSpec((1, tm, kw), f_all),
            pl.BlockSpec((1, tm, d), f_all), pl.BlockSpec((1, tm, kw), f_all),
            pl.BlockSpec((1, tm, kw), b_lat), pl.BlockSpec((1, tm, kw), b_all),
            pl.BlockSpec((1, tm, d), b_all), pl.BlockSpec((1, tm, kw), b_all),
        ],
        out_specs=[pl.BlockSpec((1, tm, d), f_lat), pl.BlockSpec((1, tm, d), b_lat)],
        out_shape=[jax.ShapeDtypeStruct((b_, l_, d), BF16), jax.ShapeDtypeStruct((b_, l_, d), BF16)],
        scratch_shapes=[pltpu.VMEM((GLA_HEADS, dk, dv), F32), pltpu.VMEM((GLA_HEADS, dk, dv), F32)],
        compiler_params=pltpu.CompilerParams(dimension_semantics=("arbitrary", "arbitrary"),
                                             vmem_limit_bytes=VMEM_LIMIT),
        name="gla",
    )(gq, gk, gv, cumf, gq, gk, gv, cumb)


def _attn_kernel(tk, scal_ref, q_ref, k_ref, vt_ref, o_ref, acc1, acc2, m1, l1, m2, l2):
    tq = q_ref.shape[1]
    nk = k_ref.shape[1] // tk
    q = q_ref[0]
    lane = lax.broadcasted_iota(jnp.int32, q.shape, 1)
    zero = jnp.zeros_like(q)
    q1 = jnp.where(lane < DIFF_DK, q, zero)
    q2 = jnp.where(lane >= DIFF_DK, q, zero)
    streams = ((q1, acc1, m1, l1), (q2, acc2, m2, l2))
    for r in (acc1, acc2, l1, l2):
        r[...] = jnp.zeros_like(r)

    def tiles(j):
        off = pl.multiple_of(j * tk, tk)
        return k_ref[0, pl.ds(off, tk), :], vt_ref[0, :, pl.ds(off, tk)]

    def bounded_body(j, carry):
        kk, vt = tiles(j)
        for qq, acc, _, l in streams:
            p = jnp.exp2(_dot_nt(kk, qq))
            l[...] += jnp.sum(p.reshape(tk // 8, 8, tq), axis=0)
            acc[...] += _dot(vt, p.astype(BF16))
        return carry

    def online_body(j, carry):
        kk, vt = tiles(j)
        for qq, acc, m, l in streams:
            s = _dot_nt(kk, qq)
            m_new = jnp.maximum(m[...], jnp.max(s, axis=0, keepdims=True))
            alpha = jnp.exp2(m[...] - m_new)
            p = jnp.exp2(s - m_new)
            l[0:1] = alpha * l[0:1] + jnp.sum(p, axis=0, keepdims=True)
            acc[...] = alpha * acc[...] + _dot(vt, p.astype(BF16))
            m[...] = m_new
        return carry

    bounded = scal_ref[1] != 0.0

    @pl.when(bounded)
    def _():
        lax.fori_loop(0, nk, bounded_body, 0, unroll=True)

    @pl.when(jnp.logical_not(bounded))
    def _():
        m1[...] = jnp.full_like(m1, -1e30)
        m2[...] = jnp.full_like(m2, -1e30)
        lax.fori_loop(0, nk, online_body, 0)

    inv1 = 1.0 / jnp.sum(l1[...], axis=0, keepdims=True)
    inv2 = 1.0 / jnp.sum(l2[...], axis=0, keepdims=True)
    o = acc1[...] * inv1 - scal_ref[0] * (acc2[...] * inv2)
    o = o * lax.rsqrt(jnp.mean(o * o, axis=0, keepdims=True) + EPS)
    o_ref[0] = o.T.astype(o_ref.dtype)


def _attn_call(scal, dq, dk, dvt, tq, tk):
    b_, l_, d = dq.shape
    lt = dk.shape[1]
    nh = d // LANES
    assert l_ % tq == 0 and lt % tk == 0
    return pl.pallas_call(
        functools.partial(_attn_kernel, tk),
        grid=(b_, nh, l_ // tq),
        in_specs=[
            pl.BlockSpec(memory_space=pltpu.SMEM),
            pl.BlockSpec((1, tq, LANES), lambda b, h, i: (b, i, h)),
            pl.BlockSpec((1, lt, LANES), lambda b, h, i: (b, 0, h)),
            pl.BlockSpec((1, LANES, lt), lambda b, h, i: (b, h, 0)),
        ],
        out_specs=pl.BlockSpec((1, tq, LANES), lambda b, h, i: (b, i, h)),
        out_shape=jax.ShapeDtypeStruct((b_, l_, d), BF16),
        scratch_shapes=[pltpu.VMEM((LANES, tq), F32), pltpu.VMEM((LANES, tq), F32),
                        pltpu.VMEM((1, tq), F32), pltpu.VMEM((8, tq), F32),
                        pltpu.VMEM((1, tq), F32), pltpu.VMEM((8, tq), F32)],
        compiler_params=pltpu.CompilerParams(
            dimension_semantics=("arbitrary", "arbitrary", "arbitrary"),
            vmem_limit_bytes=VMEM_LIMIT),
        name="diffattn",
    )(scal, dq, dk, dvt)


def _combine_kernel(x_ref, g_ref, of_ref, ob_ref, od_ref, gg_ref, dg_ref, mg_ref, md_ref,
                    ggain_ref, dgain_ref, wg_ref, wd_ref, wo_ref, out_ref):
    d = x_ref.shape[2]
    dv = d // GLA_HEADS
    og = of_ref[0].astype(F32) + ob_ref[0].astype(F32)
    normed = []
    for h in range(GLA_HEADS):
        sl = og[:, h * dv:(h + 1) * dv]
        normed.append(sl * lax.rsqrt(jnp.mean(sl * sl, axis=-1, keepdims=True) + EPS))
    a = jnp.concatenate(normed, axis=1) * ggain_ref[...] * _silu(gg_ref[0].astype(F32))
    bb = od_ref[0].astype(F32) * dgain_ref[...] * _silu(dg_ref[0].astype(F32))
    y = (jax.nn.sigmoid(mg_ref[0].astype(F32)) * _dot(a.astype(BF16), wg_ref[...])
         + jax.nn.sigmoid(md_ref[0].astype(F32)) * _dot(bb.astype(BF16), wd_ref[...]))
    out_ref[0] = x_ref[0] + g_ref[0] * _dot(y.astype(BF16), wo_ref[...])


def _combine_call(x, mod3, o_f, o_b, o_d, gg, dg, mg, md, ggain, dgain, wg, wd, wo, tm):
    b_, l_, d = x.shape
    row = lambda b, i: (b, i, 0)
    const2 = lambda b, i: (0, 0)
    once = pl.Buffered(1)
    big = pl.BlockSpec((1, tm, d), row)
    wspec = pl.BlockSpec((d, d), const2, pipeline_mode=once)
    return pl.pallas_call(
        _combine_kernel,
        grid=(b_, l_ // tm),
        in_specs=[big, pl.BlockSpec((1, 1, d), lambda b, i: (b, 0, 2)),
                  big, big, big, big, big, big, big,
                  pl.BlockSpec((1, d), const2, pipeline_mode=once),
                  pl.BlockSpec((1, d), const2, pipeline_mode=once),
                  wspec, wspec, wspec],
        out_specs=big,
        out_shape=jax.ShapeDtypeStruct(x.shape, x.dtype),
        compiler_params=pltpu.CompilerParams(dimension_semantics=("arbitrary", "arbitrary"),
                                             vmem_limit_bytes=VMEM_LIMIT),
        name="combine",
    )(x, mod3, o_f, o_b, o_d, gg, dg, mg, md, ggain, dgain, wg, wd, wo)


def _rope_tables(n_tokens, n_ctx):
    f32 = np.float32
    rows = n_tokens // GRID_W
    inv_freq = f32(ROPE_BASE) ** (-np.arange(0, ROPE_AXIS_DIM, 2, dtype=f32) / f32(ROPE_AXIS_DIM))
    ang_r = (np.arange(rows, dtype=f32)[:, None] * inv_freq).astype(f32)
    ang_c = (np.arange(GRID_W, dtype=f32)[:, None] * inv_freq).astype(f32)
    sign = np.where(np.arange(ROPE_AXIS_DIM) < ROPE_AXIS_DIM // 2, -1.0, 1.0)
    two = lambda t: np.concatenate([t, t], axis=-1).astype(f32)

    def table(fn, sgn):
        tr = jnp.broadcast_to(jnp.asarray(two(fn(ang_r)) * sgn, F32)[:, None, :], (rows, GRID_W, ROPE_AXIS_DIM))
        tc = jnp.broadcast_to(jnp.asarray(two(fn(ang_c)) * sgn, F32)[None, :, :], (rows, GRID_W, ROPE_AXIS_DIM))
        return jnp.concatenate([tr, tc, tr, tc], axis=-1).reshape(n_tokens, LANES)

    cos = jnp.concatenate([jnp.ones((n_ctx, LANES), F32), table(np.cos, 1.0)], axis=0)
    sin = jnp.concatenate([jnp.zeros((n_ctx, LANES), F32), table(np.sin, sign)], axis=0)
    return cos, sin


def _block_tri(n, upper):
    i = np.arange(n)
    same = (i[:, None] // GLA_CHUNK) == (i[None, :] // GLA_CHUNK)
    tri = (i[None, :] >= i[:, None]) if upper else (i[:, None] >= i[None, :])
    return jnp.asarray((same & tri).astype(np.float32)).astype(BF16)


def _attn_tiles(l_, lt):
    tq = 512 if l_ % 512 == 0 else l_
    for tk in (768, 512, 384, 256, 128):
        if lt % tk == 0:
            return tq, tk
    raise ValueError("key length must be a multiple of 128")


def kernel(x, c, ctx, c_ctx, w_ada, b_ada, w_in, gla_w_decay, gla_b_decay, gla_norm, diff_q_norm,
           diff_k_norm, diff_lambda, diff_norm, w_br_gla, w_br_diff, w_out):
    b_, l_, d = x.shape
    n_ctx = ctx.shape[1]
    depth = w_ada.shape[0]
    assert depth == 1, "single-layer block"
    layer = 0
    lam_init = 0.8 - 0.6 * math.exp(-0.3 * layer)
    half_k = d // 2

    cc = jnp.zeros((8, d), F32).at[:b_].set(c).at[b_].set(c_ctx)
    mod, lam_tile = _mod_call(cc, w_ada[layer], b_ada[layer][None, :], diff_lambda[layer], lam_init)
    mod3 = mod.reshape(8, 1, 3 * d)
    lam = lam_tile[0, :1]

    w = w_in[layer]
    sizes = (half_k, half_k, d, d, 2 * GLA_RANK, d, d, d, d, d, d)
    starts = np.concatenate([[0], np.cumsum(sizes)])
    col = lambda i: w[:, int(starts[i]):int(starts[i + 1])]
    names = ("gq", "gk", "gv", "gg", "dq", "dk", "dv", "dg", "mg", "md", "lr")
    order = (0, 1, 2, 3, 5, 6, 7, 8, 9, 10)
    pieces = [col(i) for i in order] + [col(4)] * 3 + [jnp.zeros((d, LANES - 6 * GLA_RANK), w.dtype)]
    w_p = jnp.concatenate(pieces, axis=1).astype(BF16)
    widths = [sizes[i] for i in order] + [LANES]
    offs, pos = {}, 0
    for name, wd_ in zip(names, widths):
        offs[name] = (pos, pos + wd_)
        pos += wd_

    zk = jnp.zeros((GLA_RANK, half_k), F32)
    wd_f32 = jnp.concatenate([jnp.concatenate([gla_w_decay[layer, 0], zk], axis=1),
                              jnp.concatenate([zk, gla_w_decay[layer, 1]], axis=1)], axis=0)
    wd_hi = wd_f32.astype(BF16)
    wd_lo = (wd_f32 - wd_hi.astype(F32)).astype(BF16)
    wdec = jnp.concatenate([wd_hi, wd_lo, wd_hi,
                            jnp.zeros((LANES - 6 * GLA_RANK, 2 * half_k), BF16)], axis=0)
    bdec = jnp.concatenate([gla_b_decay[layer, 0], gla_b_decay[layer, 1]])[None, :]
    reps = d // DIFF_DK
    qgain = (jnp.tile(diff_q_norm[layer], reps) * (DIFF_DK ** -0.5 * LOG2E))[None, :]
    kgain = jnp.tile(diff_k_norm[layer], reps)[None, :]
    cos, sin_signed = _rope_tables(l_, n_ctx)
    trif = _block_tri(ROW_TILE, upper=False)
    trib = _block_tri(ROW_TILE, upper=True)
    gi = np.arange(256)
    gsum = jnp.asarray((gi[:, None] // DIFF_DK == gi[None, :] // DIFF_DK).astype(np.float32)).astype(BF16)

    (gq, gk, gv, cumf, cumb, gg, dg, mg, md, dq, dk, dvt) = _proj_call(
        x, ctx, mod3, w_p, offs, wdec, bdec, qgain, kgain, cos, sin_signed, trif, trib, gsum)

    o_f, o_b = _gla_call(gq, gk, gv, cumf, cumb)

    tq, tk = _attn_tiles(l_, l_ + n_ctx)
    score_bound = DIFF_DK ** 0.5 * jnp.max(jnp.abs(diff_q_norm[layer])) * jnp.max(jnp.abs(diff_k_norm[layer]))
    bounded = (score_bound * BOUND_MARGIN <= SCORE_BOUND).astype(F32)
    scal = jnp.concatenate([lam, bounded[None]])
    o_d = _attn_call(scal, dq, dk, dvt, tq, tk)

    ggain = jnp.tile(gla_norm[layer], GLA_HEADS)[None, :]
    dgain = (jnp.tile(diff_norm[layer], d // diff_norm.shape[1]) * (1.0 - lam_init))[None, :]
    tm = 512 if l_ % 512 == 0 else l_
    return _combine_call(x, mod3, o_f, o_b, o_d, gg, dg, mg, md, ggain, dgain,
                         w_br_gla[layer].astype(BF16), w_br_diff[layer].astype(BF16),
                         w_out[layer].astype(BF16), tm)
```

```python
import functools
import math

import jax
import jax.numpy as jnp
import numpy as np
from jax import lax
from jax.experimental import pallas as pl
from jax.experimental.pallas import tpu as pltpu

F32 = jnp.float32
BF16 = jnp.bfloat16
HIGHEST = lax.Precision.HIGHEST

EPS = 1e-6
GRID_W = 64
GLA_HEADS = 4
GLA_RANK = 16
GLA_TAU = 16.0
GLA_CHUNK = 64
DIFF_DK = 64
ROPE_BASE = 10000.0
ROPE_AXIS_DIM = DIFF_DK // 2
LANES = 128
ROW_TILE = 256
VMEM_LIMIT = 56 * 1024 * 1024
LOG2E = math.log2(math.e)
SCORE_BOUND = 40.0
BOUND_MARGIN = 1.02


def _dot(a, b):
    return jnp.dot(a, b, preferred_element_type=F32)


def _dot_nt(a, b):
    return lax.dot_general(a, b, (((1,), (1,)), ((), ())), preferred_element_type=F32)


def _dot_exact(a, b):
    return lax.dot_general(a, b, (((1,), (0,)), ((), ())), precision=HIGHEST,
                           preferred_element_type=F32)


def _split_bf16(a):
    hi = a.astype(BF16)
    return hi, (a - hi.astype(F32)).astype(BF16)


def _silu(t):
    return t * jax.nn.sigmoid(t)


def _mod_kernel(lam_init, cc_ref, w_ref, b_ref, lamp_ref, mod_ref, lam_ref):
    cc = cc_ref[...]
    mod_ref[...] = _dot_exact(_silu(cc), w_ref[...]) + b_ref[...]
    lp = lamp_ref[...]
    e1 = jnp.exp(jnp.sum(lp[0:1] * lp[1:2], axis=-1, keepdims=True))
    e2 = jnp.exp(jnp.sum(lp[2:3] * lp[3:4], axis=-1, keepdims=True))
    lam_ref[...] = jnp.broadcast_to(e1 - e2 + lam_init, lam_ref.shape)


def _mod_call(cc, w_ada, b_ada, lam_params, lam_init):
    d = cc.shape[1]
    n = w_ada.shape[1]
    return pl.pallas_call(
        functools.partial(_mod_kernel, lam_init),
        grid=(n // d,),
        in_specs=[
            pl.BlockSpec((8, d), lambda j: (0, 0)),
            pl.BlockSpec((d, d), lambda j: (0, j)),
            pl.BlockSpec((1, d), lambda j: (0, j)),
            pl.BlockSpec(lam_params.shape, lambda j: (0, 0)),
        ],
        out_specs=[
            pl.BlockSpec((8, d), lambda j: (0, j)),
            pl.BlockSpec((8, LANES), lambda j: (0, 0)),
        ],
        out_shape=[jax.ShapeDtypeStruct((8, n), F32), jax.ShapeDtypeStruct((8, LANES), F32)],
        compiler_params=pltpu.CompilerParams(dimension_semantics=("arbitrary",)),
        name="mod",
    )(cc, w_ada, b_ada, lam_params)


def _group_norm_rope(p, gsum_ref, gain_ref, cos, sin_signed, lo_mask):
    tm, d = p.shape
    sq = (p * p).astype(BF16)
    slabs = []
    for s in range(d // 256):
        slabs.append(_dot(sq[:, s * 256:(s + 1) * 256], gsum_ref[...]))
    ss = jnp.concatenate(slabs, axis=1)
    n = p * lax.rsqrt(ss * (1.0 / DIFF_DK) + EPS) * gain_ref[...]
    outs = []
    for h in range(d // LANES):
        sl = n[:, h * LANES:(h + 1) * LANES]
        up = pltpu.roll(sl, LANES - ROPE_AXIS_DIM // 2, axis=1)
        dn = pltpu.roll(sl, ROPE_AXIS_DIM // 2, axis=1)
        outs.append(sl * cos + jnp.where(lo_mask, up, dn) * sin_signed)
    return jnp.concatenate(outs, axis=1)


def _proj_kernel(offs, x_ref, ctx_ref, shift_ref, scale_ref, w_ref, wdec_ref, bdec_ref,
                 qgain_ref, kgain_ref, cos_ref, sin_ref, trif_ref, trib_ref, gsum_ref,
                 gq_ref, gk_ref, gv_ref, gvt_ref, cf_ref, cb_ref, gg_ref, dg_ref, mg_ref, md_ref,
                 dq_ref, dk_ref, dvt_ref):
    t = pl.program_id(1)
    latent = t > 0
    xin = jnp.where(latent, x_ref[0], ctx_ref[0])
    ms = jnp.mean(xin * xin, axis=-1, keepdims=True)
    h = xin * lax.rsqrt(ms + EPS) * (1.0 + scale_ref[0]) + shift_ref[0]
    hb = h.astype(BF16)

    def proj(name):
        a, b = offs[name]
        return _dot(hb, w_ref[:, a:b])

    dk_gla = gq_ref.shape[2] // GLA_HEADS
    gq_ref[0] = proj("gq") * dk_gla ** -0.5
    gk_ref[0] = proj("gk")
    gv = proj("gv")
    gv_ref[0] = gv.astype(BF16)
    gvt_ref[0] = gv.T.astype(BF16)

    lr3 = proj("lr")
    lr_hi, lr_lo = _split_bf16(lr3)
    lane = lax.broadcasted_iota(jnp.int32, lr3.shape, 1)
    z = _dot(jnp.where(lane < 4 * GLA_RANK, lr_hi, lr_lo), wdec_ref[...]) + bdec_ref[...]
    la = (jnp.minimum(z, 0.0) - jnp.log1p(jnp.exp(-jnp.abs(z)))) * (1.0 / GLA_TAU)
    half = la.shape[1] // 2
    la_hi, la_lo = _split_bf16(la)
    cf_ref[0] = _dot(trif_ref[...], la_hi[:, :half]) + _dot(trif_ref[...], la_lo[:, :half])
    cb_ref[0] = _dot(trib_ref[...], la_hi[:, half:]) + _dot(trib_ref[...], la_lo[:, half:])

    gg_ref[0] = proj("gg").astype(BF16)
    dg_ref[0] = proj("dg").astype(BF16)
    mg_ref[0] = proj("mg").astype(BF16)
    md_ref[0] = proj("md").astype(BF16)

    cos = cos_ref[...]
    sin_signed = sin_ref[...]
    lane = lax.broadcasted_iota(jnp.int32, cos.shape, 1)
    lo_mask = (lane % ROPE_AXIS_DIM) < (ROPE_AXIS_DIM // 2)
    dq_ref[0] = _group_norm_rope(proj("dq"), gsum_ref, qgain_ref, cos, sin_signed, lo_mask).astype(BF16)
    dk_ref[0] = _group_norm_rope(proj("dk"), gsum_ref, kgain_ref, cos, sin_signed, lo_mask).astype(BF16)
    dvt_ref[0] = proj("dv").T.astype(BF16)


def _proj_call(x, ctx, mod3, w_p, offs, wdec, bdec, qgain, kgain, cos, sin_signed, trif, trib, gsum):
    b_, l_, d = x.shape
    tm = ROW_TILE
    assert ctx.shape[1] == tm and l_ % tm == 0
    nt = l_ // tm + 1
    lt = l_ + tm
    n_ctx_row = b_
    half_k = offs["gq"][1] - offs["gq"][0]

    lat = lambda b, t: (b, jnp.maximum(t - 1, 0), 0)
    allr = lambda b, t: (b, t, 0)
    const2 = lambda b, t: (0, 0)
    once = pl.Buffered(1)

    def mod_spec(col):
        return pl.BlockSpec((1, 1, d), lambda b, t: (jnp.where(t == 0, n_ctx_row, b), 0, col))

    in_specs = [
        pl.BlockSpec((1, tm, d), lat),
        pl.BlockSpec((1, tm, d), lambda b, t: (b, 0, 0)),
        mod_spec(0), mod_spec(1),
        pl.BlockSpec(w_p.shape, const2, pipeline_mode=once),
        pl.BlockSpec(wdec.shape, const2, pipeline_mode=once),
        pl.BlockSpec(bdec.shape, const2, pipeline_mode=once),
        pl.BlockSpec(qgain.shape, const2, pipeline_mode=once),
        pl.BlockSpec(kgain.shape, const2, pipeline_mode=once),
        pl.BlockSpec((tm, LANES), lambda b, t: (t, 0)),
        pl.BlockSpec((tm, LANES), lambda b, t: (t, 0)),
        pl.BlockSpec(trif.shape, const2, pipeline_mode=once),
        pl.BlockSpec(trib.shape, const2, pipeline_mode=once),
        pl.BlockSpec(gsum.shape, const2, pipeline_mode=once),
    ]
    out_specs = [
        pl.BlockSpec((1, tm, half_k), lat),
        pl.BlockSpec((1, tm, half_k), allr),
        pl.BlockSpec((1, tm, d), allr),
        pl.BlockSpec((1, d, tm), lambda b, t: (b, 0, t)),
        pl.BlockSpec((1, tm, half_k), allr),
        pl.BlockSpec((1, tm, half_k), allr),
        pl.BlockSpec((1, tm, d), lat),
        pl.BlockSpec((1, tm, d), lat),
        pl.BlockSpec((1, tm, d), lat),
        pl.BlockSpec((1, tm, d), lat),
        pl.BlockSpec((1, tm, d), lat),
        pl.BlockSpec((1, tm, d), allr),
        pl.BlockSpec((1, d, tm), lambda b, t: (b, 0, t)),
    ]
    sd = jax.ShapeDtypeStruct
    out_shape = [
        sd((b_, l_, half_k), F32), sd((b_, lt, half_k), F32), sd((b_, lt, d), BF16), sd((b_, d, lt), BF16),
        sd((b_, lt, half_k), F32), sd((b_, lt, half_k), F32),
        sd((b_, l_, d), BF16), sd((b_, l_, d), BF16), sd((b_, l_, d), BF16), sd((b_, l_, d), BF16),
        sd((b_, l_, d), BF16), sd((b_, lt, d), BF16), sd((b_, d, lt), BF16),
    ]
    return pl.pallas_call(
        functools.partial(_proj_kernel, offs),
        grid=(b_, nt),
        in_specs=in_specs, out_specs=out_specs, out_shape=out_shape,
        compiler_params=pltpu.CompilerParams(dimension_semantics=("arbitrary", "arbitrary"),
                                             vmem_limit_bytes=VMEM_LIMIT),
        name="proj",
    )(x, ctx, mod3, mod3, w_p, wdec, bdec, qgain, kgain, cos, sin_signed, trif, trib, gsum)


def _chunk_rows(rows, dk):
    return jnp.concatenate([jnp.broadcast_to(r, (GLA_CHUNK, dk)) for r in rows], axis=0)


def _gla_direction(forward, q_ref, k_ref, v_ref, vt_ref, c_ref, o_ref, st_ref):
    tm = k_ref.shape[1]
    nh = GLA_HEADS
    dk = k_ref.shape[2] // nh
    dv = v_ref.shape[2] // nh
    nc = tm // GLA_CHUNK
    half = GLA_CHUNK // 2
    ri = lax.broadcasted_iota(jnp.int32, (tm, tm), 0)
    ci = lax.broadcasted_iota(jnp.int32, (tm, tm), 1)
    same = (ri // GLA_CHUNK) == (ci // GLA_CHUNK)
    mask = same & ((ri >= ci) if forward else (ci >= ri))
    row_chunk = lax.broadcasted_iota(jnp.int32, (tm, dk), 0) // GLA_CHUNK
    order = list(range(nc)) if forward else list(reversed(range(nc)))
    zrow = jnp.zeros((1, dk), F32)
    for h in range(nh):
        ks = slice(h * dk, (h + 1) * dk)
        vs = slice(h * dv, (h + 1) * dv)
        cum = c_ref[0, :, ks]
        q = q_ref[0, :, ks]
        k = k_ref[0, :, ks]
        v = v_ref[0, :, vs]
        vt = vt_ref[0, vs, :]
        last, ref = [], []
        for c in range(nc):
            base = c * GLA_CHUNK
            li = base + GLA_CHUNK - 1 if forward else base
            ce = base + half - 1 if forward else base + half
            last.append(cum[li:li + 1])
            ref.append(cum[ce:ce + 1])
        lastb = _chunk_rows(last, dk)
        refb = _chunk_rows(ref, dk)
        qe = (q * jnp.exp(cum - refb)).astype(BF16)
        ke = (k * jnp.exp(refb - cum)).astype(BF16)
        a = jnp.where(mask, _dot_nt(qe, ke), 0.0).astype(BF16)
        o_intra = _dot(a, v)
        p_start, p_end, run = {}, {}, zrow
        for c in order:
            p_start[c] = run
            run = run + last[c]
            p_end[c] = run
        p_total = run
        qd = q * jnp.exp(cum)
        kd = k * jnp.exp(lastb - cum)
        sources = order[:-1]
        k_cols = [jnp.where(row_chunk == s, kd, 0.0).astype(BF16) for s in sources]
        k_cols.append((kd * jnp.exp(_chunk_rows([p_total - p_end[c] for c in range(nc)], dk))).astype(BF16))
        ut = _dot(vt, jnp.concatenate(k_cols, axis=1))
        st = st_ref[h]
        wt = jnp.concatenate([st, ut[:, :(nc - 1) * dk]], axis=1).astype(BF16)
        q_cols = [(qd * jnp.exp(_chunk_rows([p_start[c] for c in range(nc)], dk))).astype(BF16)]
        for i, s in enumerate(sources):
            later = order[i + 1:]
            fac = [jnp.exp(p_start[c] - p_end[s]) if c in later else zrow for c in range(nc)]
            q_cols.append((qd * _chunk_rows(fac, dk)).astype(BF16))
        o_inter = _dot_nt(jnp.concatenate(q_cols, axis=1), wt)
        st_ref[h] = st * jnp.exp(p_total) + ut[:, (nc - 1) * dk:]
        o_ref[0, :, vs] = (o_intra + o_inter).astype(o_ref.dtype)


def _gla_kernel(qf_ref, kf_ref, vf_ref, vtf_ref, cf_ref, qb_ref, kb_ref, vb_ref, vtb_ref, cb_ref,
                of_ref, ob_ref, sf_ref, sb_ref):
    @pl.when(pl.program_id(1) == 0)
    def _():
        sf_ref[...] = jnp.zeros_like(sf_ref)
        sb_ref[...] = jnp.zeros_like(sb_ref)

    _gla_direction(True, qf_ref, kf_ref, vf_ref, vtf_ref, cf_ref, of_ref, sf_ref)
    _gla_direction(False, qb_ref, kb_ref, vb_ref, vtb_ref, cb_ref, ob_ref, sb_ref)


def _gla_call(gq, gk, gv, gvt, cumf, cumb):
    b_, l_, kw = gq.shape
    lt = gk.shape[1]
    d = gv.shape[2]
    tm = ROW_TILE
    nt = lt // tm
    nlat = l_ // tm
    dk = kw // GLA_HEADS
    dv = d // GLA_HEADS

    bwd_blk = lambda s: jnp.where(s == 0, 0, nt - s)
    f_lat = lambda b, s: (b, jnp.maximum(s - 1, 0), 0)
    f_all = lambda b, s: (b, s, 0)
    b_lat = lambda b, s: (b, jnp.where(s == 0, nlat - 1, nlat - s), 0)
    b_all = lambda b, s: (b, bwd_blk(s), 0)
    return pl.pallas_call(
        _gla_kernel,
        grid=(b_, nt),
        in_specs=[
            pl.BlockSpec((1, tm, kw), f_lat), pl.BlockSpec((1, tm, kw), f_all),
            pl.BlockSpec((1, tm, d), f_all), pl.BlockSpec((1, d, tm), lambda b, s: (b, 0, s)),
            pl.BlockSpec((1, tm, kw), f_all),
            pl.BlockSpec((1, tm, kw), b_lat), pl.BlockSpec((1, tm, kw), b_all),
            pl.BlockSpec((1, tm, d), b_all), pl.BlockSpec((1, d, tm), lambda b, s: (b, 0, bwd_blk(s))),
            pl.BlockSpec((1, tm, kw), b_all),
        ],
        out_specs=[pl.BlockSpec((1, tm, d), f_lat), pl.BlockSpec((1, tm, d), b_lat)],
        out_shape=[jax.ShapeDtypeStruct((b_, l_, d), BF16), jax.ShapeDtypeStruct((b_, l_, d), BF16)],
        scratch_shapes=[pltpu.VMEM((GLA_HEADS, dv, dk), F32), pltpu.VMEM((GLA_HEADS, dv, dk), F32)],
        compiler_params=pltpu.CompilerParams(dimension_semantics=("arbitrary", "arbitrary"),
                                             vmem_limit_bytes=VMEM_LIMIT),
        name="gla",
    )(gq, gk, gv, gvt, cumf, gq, gk, gv, gvt, cumb)


def _attn_kernel(tk, scal_ref, q_ref, k_ref, vt_ref, o_ref, acc1, acc2, m1, l1, m2, l2):
    tq = q_ref.shape[1]
    nk = k_ref.shape[1] // tk
    q = q_ref[0]
    lane = lax.broadcasted_iota(jnp.int32, q.shape, 1)
    zero = jnp.zeros_like(q)
    q1 = jnp.where(lane < DIFF_DK, q, zero)
    q2 = jnp.where(lane >= DIFF_DK, q, zero)
    streams = ((q1, acc1, m1, l1), (q2, acc2, m2, l2))
    for r in (acc1, acc2, l1, l2):
        r[...] = jnp.zeros_like(r)

    def tiles(j):
        off = pl.multiple_of(j * tk, tk)
        return k_ref[0, pl.ds(off, tk), :], vt_ref[0, :, pl.ds(off, tk)]

    def bounded_body(j, carry):
        kk, vt = tiles(j)
        ps = [jnp.exp2(_dot_nt(kk, qq)) for qq, _, _, _ in streams]
        for p, (_, acc, _, l) in zip(ps, streams):
            l[...] += jnp.sum(p.reshape(tk // 8, 8, tq), axis=0)
            acc[...] += _dot(vt, p.astype(BF16))
        return carry

    def online_body(j, carry):
        kk, vt = tiles(j)
        for qq, acc, m, l in streams:
            s = _dot_nt(kk, qq)
            m_new = jnp.maximum(m[...], jnp.max(s, axis=0, keepdims=True))
            alpha = jnp.exp2(m[...] - m_new)
            p = jnp.exp2(s - m_new)
            l[0:1] = alpha * l[0:1] + jnp.sum(p, axis=0, keepdims=True)
            acc[...] = alpha * acc[...] + _dot(vt, p.astype(BF16))
            m[...] = m_new
        return carry

    bounded = scal_ref[1] != 0.0

    @pl.when(bounded)
    def _():
        lax.fori_loop(0, nk, bounded_body, 0, unroll=True)

    @pl.when(jnp.logical_not(bounded))
    def _():
        m1[...] = jnp.full_like(m1, -1e30)
        m2[...] = jnp.full_like(m2, -1e30)
        lax.fori_loop(0, nk, online_body, 0)

    inv1 = 1.0 / jnp.sum(l1[...], axis=0, keepdims=True)
    inv2 = 1.0 / jnp.sum(l2[...], axis=0, keepdims=True)
    o = acc1[...] * inv1 - scal_ref[0] * (acc2[...] * inv2)
    o = o * lax.rsqrt(jnp.mean(o * o, axis=0, keepdims=True) + EPS)
    o_ref[0] = o.T.astype(o_ref.dtype)


def _attn_call(scal, dq, dk, dvt, tq, tk):
    b_, l_, d = dq.shape
    lt = dk.shape[1]
    nh = d // LANES
    assert l_ % tq == 0 and lt % tk == 0
    return pl.pallas_call(
        functools.partial(_attn_kernel, tk),
        grid=(b_, nh, l_ // tq),
        in_specs=[
            pl.BlockSpec(memory_space=pltpu.SMEM),
            pl.BlockSpec((1, tq, LANES), lambda b, h, i: (b, i, h)),
            pl.BlockSpec((1, lt, LANES), lambda b, h, i: (b, 0, h)),
            pl.BlockSpec((1, LANES, lt), lambda b, h, i: (b, h, 0)),
        ],
        out_specs=pl.BlockSpec((1, tq, LANES), lambda b, h, i: (b, i, h)),
        out_shape=jax.ShapeDtypeStruct((b_, l_, d), BF16),
        scratch_shapes=[pltpu.VMEM((LANES, tq), F32), pltpu.VMEM((LANES, tq), F32),
                        pltpu.VMEM((1, tq), F32), pltpu.VMEM((8, tq), F32),
                        pltpu.VMEM((1, tq), F32), pltpu.VMEM((8, tq), F32)],
        compiler_params=pltpu.CompilerParams(
            dimension_semantics=("arbitrary", "arbitrary", "arbitrary"),
            vmem_limit_bytes=VMEM_LIMIT),
        name="diffattn",
    )(scal, dq, dk, dvt)


def _combine_kernel(x_ref, g_ref, of_ref, ob_ref, od_ref, gg_ref, dg_ref, mg_ref, md_ref,
                    ggain_ref, dgain_ref, wg_ref, wd_ref, wo_ref, out_ref):
    d = x_ref.shape[2]
    dv = d // GLA_HEADS
    og = of_ref[0].astype(F32) + ob_ref[0].astype(F32)
    normed = []
    for h in range(GLA_HEADS):
        sl = og[:, h * dv:(h + 1) * dv]
        normed.append(sl * lax.rsqrt(jnp.mean(sl * sl, axis=-1, keepdims=True) + EPS))
    a = jnp.concatenate(normed, axis=1) * ggain_ref[...] * _silu(gg_ref[0].astype(F32))
    bb = od_ref[0].astype(F32) * dgain_ref[...] * _silu(dg_ref[0].astype(F32))
    y = (jax.nn.sigmoid(mg_ref[0].astype(F32)) * _dot(a.astype(BF16), wg_ref[...])
         + jax.nn.sigmoid(md_ref[0].astype(F32)) * _dot(bb.astype(BF16), wd_ref[...]))
    out_ref[0] = x_ref[0] + g_ref[0] * _dot(y.astype(BF16), wo_ref[...])


def _combine_call(x, mod3, o_f, o_b, o_d, gg, dg, mg, md, ggain, dgain, wg, wd, wo, tm):
    b_, l_, d = x.shape
    row = lambda b, i: (b, i, 0)
    const2 = lambda b, i: (0, 0)
    once = pl.Buffered(1)
    big = pl.BlockSpec((1, tm, d), row)
    wspec = pl.BlockSpec((d, d), const2, pipeline_mode=once)
    return pl.pallas_call(
        _combine_kernel,
        grid=(b_, l_ // tm),
        in_specs=[big, pl.BlockSpec((1, 1, d), lambda b, i: (b, 0, 2)),
                  big, big, big, big, big, big, big,
                  pl.BlockSpec((1, d), const2, pipeline_mode=once),
                  pl.BlockSpec((1, d), const2, pipeline_mode=once),
                  wspec, wspec, wspec],
        out_specs=big,
        out_shape=jax.ShapeDtypeStruct(x.shape, x.dtype),
        compiler_params=pltpu.CompilerParams(dimension_semantics=("arbitrary", "arbitrary"),
                                             vmem_limit_bytes=VMEM_LIMIT),
        name="combine",
    )(x, mod3, o_f, o_b, o_d, gg, dg, mg, md, ggain, dgain, wg, wd, wo)


def _rope_tables(n_tokens, n_ctx):
    f32 = np.float32
    rows = n_tokens // GRID_W
    inv_freq = f32(ROPE_BASE) ** (-np.arange(0, ROPE_AXIS_DIM, 2, dtype=f32) / f32(ROPE_AXIS_DIM))
    ang_r = (np.arange(rows, dtype=f32)[:, None] * inv_freq).astype(f32)
    ang_c = (np.arange(GRID_W, dtype=f32)[:, None] * inv_freq).astype(f32)
    sign = np.where(np.arange(ROPE_AXIS_DIM) < ROPE_AXIS_DIM // 2, -1.0, 1.0)
    two = lambda t: np.concatenate([t, t], axis=-1).astype(f32)

    def table(fn, sgn):
        tr = jnp.broadcast_to(jnp.asarray(two(fn(ang_r)) * sgn, F32)[:, None, :], (rows, GRID_W, ROPE_AXIS_DIM))
        tc = jnp.broadcast_to(jnp.asarray(two(fn(ang_c)) * sgn, F32)[None, :, :], (rows, GRID_W, ROPE_AXIS_DIM))
        return jnp.concatenate([tr, tc, tr, tc], axis=-1).reshape(n_tokens, LANES)

    cos = jnp.concatenate([jnp.ones((n_ctx, LANES), F32), table(np.cos, 1.0)], axis=0)
    sin = jnp.concatenate([jnp.zeros((n_ctx, LANES), F32), table(np.sin, sign)], axis=0)
    return cos, sin


def _block_tri(n, upper):
    i = np.arange(n)
    same = (i[:, None] // GLA_CHUNK) == (i[None, :] // GLA_CHUNK)
    tri = (i[None, :] >= i[:, None]) if upper else (i[:, None] >= i[None, :])
    return jnp.asarray((same & tri).astype(np.float32)).astype(BF16)


def _attn_tiles(l_, lt):
    tq = 512 if l_ % 512 == 0 else l_
    for tk in (768, 512, 384, 256, 128):
        if lt % tk == 0:
            return tq, tk
    raise ValueError("key length must be a multiple of 128")


def kernel(x, c, ctx, c_ctx, w_ada, b_ada, w_in, gla_w_decay, gla_b_decay, gla_norm, diff_q_norm,
           diff_k_norm, diff_lambda, diff_norm, w_br_gla, w_br_diff, w_out):
    b_, l_, d = x.shape
    n_ctx = ctx.shape[1]
    depth = w_ada.shape[0]
    assert depth == 1, "single-layer block"
    layer = 0
    lam_init = 0.8 - 0.6 * math.exp(-0.3 * layer)
    half_k = d // 2

    cc = jnp.zeros((8, d), F32).at[:b_].set(c).at[b_].set(c_ctx)
    mod, lam_tile = _mod_call(cc, w_ada[layer], b_ada[layer][None, :], diff_lambda[layer], lam_init)
    mod3 = mod.reshape(8, 1, 3 * d)
    lam = lam_tile[0, :1]

    w = w_in[layer]
    sizes = (half_k, half_k, d, d, 2 * GLA_RANK, d, d, d, d, d, d)
    starts = np.concatenate([[0], np.cumsum(sizes)])
    col = lambda i: w[:, int(starts[i]):int(starts[i + 1])]
    names = ("gq", "gk", "gv", "gg", "dq", "dk", "dv", "dg", "mg", "md", "lr")
    order = (0, 1, 2, 3, 5, 6, 7, 8, 9, 10)
    pieces = [col(i) for i in order] + [col(4)] * 3 + [jnp.zeros((d, LANES - 6 * GLA_RANK), w.dtype)]
    w_p = jnp.concatenate(pieces, axis=1).astype(BF16)
    widths = [sizes[i] for i in order] + [LANES]
    offs, pos = {}, 0
    for name, wd_ in zip(names, widths):
        offs[name] = (pos, pos + wd_)
        pos += wd_

    zk = jnp.zeros((GLA_RANK, half_k), F32)
    wd_f32 = jnp.concatenate([jnp.concatenate([gla_w_decay[layer, 0], zk], axis=1),
                              jnp.concatenate([zk, gla_w_decay[layer, 1]], axis=1)], axis=0)
    wd_hi = wd_f32.astype(BF16)
    wd_lo = (wd_f32 - wd_hi.astype(F32)).astype(BF16)
    wdec = jnp.concatenate([wd_hi, wd_lo, wd_hi,
                            jnp.zeros((LANES - 6 * GLA_RANK, 2 * half_k), BF16)], axis=0)
    bdec = jnp.concatenate([gla_b_decay[layer, 0], gla_b_decay[layer, 1]])[None, :]
    reps = d // DIFF_DK
    qgain = (jnp.tile(diff_q_norm[layer], reps) * (DIFF_DK ** -0.5 * LOG2E))[None, :]
    kgain = jnp.tile(diff_k_norm[layer], reps)[None, :]
    cos, sin_signed = _rope_tables(l_, n_ctx)
    trif = _block_tri(ROW_TILE, upper=False)
    trib = _block_tri(ROW_TILE, upper=True)
    gi = np.arange(256)
    gsum = jnp.asarray((gi[:, None] // DIFF_DK == gi[None, :] // DIFF_DK).astype(np.float32)).astype(BF16)

    (gq, gk, gv, gvt, cumf, cumb, gg, dg, mg, md, dq, dk, dvt) = _proj_call(
        x, ctx, mod3, w_p, offs, wdec, bdec, qgain, kgain, cos, sin_signed, trif, trib, gsum)

    o_f, o_b = _gla_call(gq, gk, gv, gvt, cumf, cumb)

    tq, tk = _attn_tiles(l_, l_ + n_ctx)
    score_bound = DIFF_DK ** 0.5 * jnp.max(jnp.abs(diff_q_norm[layer])) * jnp.max(jnp.abs(diff_k_norm[layer]))
    bounded = (score_bound * BOUND_MARGIN <= SCORE_BOUND).astype(F32)
    scal = jnp.concatenate([lam, bounded[None]])
    o_d = _attn_call(scal, dq, dk, dvt, tq, tk)

    ggain = jnp.tile(gla_norm[layer], GLA_HEADS)[None, :]
    dgain = (jnp.tile(diff_norm[layer], d // diff_norm.shape[1]) * (1.0 - lam_init))[None, :]
    tm = 512 if l_ % 512 == 0 else l_
    return _combine_call(x, mod3, o_f, o_b, o_d, gg, dg, mg, md, ggain, dgain,
                         w_br_gla[layer].astype(BF16), w_br_diff[layer].astype(BF16),
                         w_out[layer].astype(BF16), tm)
```

```python
import functools
import math

import jax
import jax.numpy as jnp
import numpy as np
from jax import lax
from jax.experimental import pallas as pl
from jax.experimental.pallas import tpu as pltpu

F32 = jnp.float32
BF16 = jnp.bfloat16
HIGHEST = lax.Precision.HIGHEST

EPS = 1e-6
GRID_W = 64
GLA_HEADS = 4
GLA_RANK = 16
GLA_TAU = 16.0
GLA_CHUNK = 64
DIFF_DK = 64
ROPE_BASE = 10000.0
ROPE_AXIS_DIM = DIFF_DK // 2
LANES = 128
ROW_TILE = 256
VMEM_LIMIT = 56 * 1024 * 1024
LOG2E = math.log2(math.e)
SCORE_BOUND = 40.0
BOUND_MARGIN = 1.02


def _dot(a, b):
    return jnp.dot(a, b, preferred_element_type=F32)


def _dot_nt(a, b):
    return lax.dot_general(a, b, (((1,), (1,)), ((), ())), preferred_element_type=F32)


def _dot_exact(a, b):
    return lax.dot_general(a, b, (((1,), (0,)), ((), ())), precision=HIGHEST,
                           preferred_element_type=F32)


def _split_bf16(a):
    hi = a.astype(BF16)
    return hi, (a - hi.astype(F32)).astype(BF16)


def _sigmoid(t):
    return 0.5 * jnp.tanh(0.5 * t) + 0.5


def _silu(t):
    return t * _sigmoid(t)


def _mod_kernel(lam_init, cc_ref, w_ref, b_ref, lamp_ref, mod_ref, lam_ref):
    cc = cc_ref[...]
    mod_ref[...] = _dot_exact(_silu(cc), w_ref[...]) + b_ref[...]
    lp = lamp_ref[...]
    e1 = jnp.exp(jnp.sum(lp[0:1] * lp[1:2], axis=-1, keepdims=True))
    e2 = jnp.exp(jnp.sum(lp[2:3] * lp[3:4], axis=-1, keepdims=True))
    lam_ref[...] = jnp.broadcast_to(e1 - e2 + lam_init, lam_ref.shape)


def _mod_call(cc, w_ada, b_ada, lam_params, lam_init):
    d = cc.shape[1]
    n = w_ada.shape[1]
    return pl.pallas_call(
        functools.partial(_mod_kernel, lam_init),
        grid=(n // d,),
        in_specs=[
            pl.BlockSpec((8, d), lambda j: (0, 0)),
            pl.BlockSpec((d, d), lambda j: (0, j)),
            pl.BlockSpec((1, d), lambda j: (0, j)),
            pl.BlockSpec(lam_params.shape, lambda j: (0, 0)),
        ],
        out_specs=[
            pl.BlockSpec((8, d), lambda j: (0, j)),
            pl.BlockSpec((8, LANES), lambda j: (0, 0)),
        ],
        out_shape=[jax.ShapeDtypeStruct((8, n), F32), jax.ShapeDtypeStruct((8, LANES), F32)],
        compiler_params=pltpu.CompilerParams(dimension_semantics=("arbitrary",)),
        name="mod",
    )(cc, w_ada, b_ada, lam_params)


def _group_norm_rope(p, gsum_ref, gain_ref, cos, sin_signed, lo_mask):
    tm, d = p.shape
    sq = (p * p).astype(BF16)
    slabs = []
    for s in range(d // 256):
        slabs.append(_dot(sq[:, s * 256:(s + 1) * 256], gsum_ref[...]))
    ss = jnp.concatenate(slabs, axis=1)
    n = p * lax.rsqrt(ss * (1.0 / DIFF_DK) + EPS) * gain_ref[...]
    outs = []
    for h in range(d // LANES):
        sl = n[:, h * LANES:(h + 1) * LANES]
        up = pltpu.roll(sl, LANES - ROPE_AXIS_DIM // 2, axis=1)
        dn = pltpu.roll(sl, ROPE_AXIS_DIM // 2, axis=1)
        outs.append(sl * cos + jnp.where(lo_mask, up, dn) * sin_signed)
    return jnp.concatenate(outs, axis=1)


def _proj_kernel(offs, x_ref, ctx_ref, shift_ref, scale_ref, w_ref, wdec_ref, bdec_ref,
                 qgain_ref, kgain_ref, cos_ref, sin_ref, trif_ref, trib_ref, gsum_ref,
                 gq_ref, gk_ref, gv_ref, gvt_ref, cf_ref, cb_ref, gg_ref, dg_ref, mg_ref, md_ref,
                 dq_ref, dk_ref, dvt_ref):
    t = pl.program_id(1)
    latent = t > 0
    xin = jnp.where(latent, x_ref[0], ctx_ref[0])
    ms = jnp.mean(xin * xin, axis=-1, keepdims=True)
    h = xin * lax.rsqrt(ms + EPS) * (1.0 + scale_ref[0]) + shift_ref[0]
    hb = h.astype(BF16)

    def proj(name):
        a, b = offs[name]
        return _dot(hb, w_ref[:, a:b])

    lr3 = proj("lr")
    lr_hi, lr_lo = _split_bf16(lr3)
    lane = lax.broadcasted_iota(jnp.int32, lr3.shape, 1)
    z = _dot(jnp.where(lane < 4 * GLA_RANK, lr_hi, lr_lo), wdec_ref[...]) + bdec_ref[...]
    la = (jnp.minimum(z, 0.0) - jnp.log(1.0 + jnp.exp(-jnp.abs(z)))) * (1.0 / GLA_TAU)
    half = la.shape[1] // 2
    la_hi, la_lo = _split_bf16(la)

    dk_gla = gq_ref.shape[2] // GLA_HEADS
    gq_ref[0] = proj("gq") * dk_gla ** -0.5
    gk_ref[0] = proj("gk")
    gv = proj("gv")
    gv_ref[0] = gv.astype(BF16)
    gvt_ref[0] = gv.T.astype(BF16)

    gg_ref[0] = _silu(proj("gg")).astype(BF16)
    dg_ref[0] = _silu(proj("dg")).astype(BF16)
    mg_ref[0] = _sigmoid(proj("mg")).astype(BF16)
    md_ref[0] = _sigmoid(proj("md")).astype(BF16)

    cf_ref[0] = _dot(trif_ref[...], la_hi[:, :half]) + _dot(trif_ref[...], la_lo[:, :half])
    cb_ref[0] = _dot(trib_ref[...], la_hi[:, half:]) + _dot(trib_ref[...], la_lo[:, half:])

    cos = cos_ref[...]
    sin_signed = sin_ref[...]
    lane = lax.broadcasted_iota(jnp.int32, cos.shape, 1)
    lo_mask = (lane % ROPE_AXIS_DIM) < (ROPE_AXIS_DIM // 2)
    dq_ref[0] = _group_norm_rope(proj("dq"), gsum_ref, qgain_ref, cos, sin_signed, lo_mask).astype(BF16)
    dk_ref[0] = _group_norm_rope(proj("dk"), gsum_ref, kgain_ref, cos, sin_signed, lo_mask).astype(BF16)
    dvt_ref[0] = proj("dv").T.astype(BF16)


def _proj_call(x, ctx, mod3, w_p, offs, wdec, bdec, qgain, kgain, cos, sin_signed, trif, trib, gsum):
    b_, l_, d = x.shape
    tm = ROW_TILE
    assert ctx.shape[1] == tm and l_ % tm == 0
    nt = l_ // tm + 1
    lt = l_ + tm
    n_ctx_row = b_
    half_k = offs["gq"][1] - offs["gq"][0]

    lat = lambda b, t: (b, jnp.maximum(t - 1, 0), 0)
    allr = lambda b, t: (b, t, 0)
    const2 = lambda b, t: (0, 0)
    once = pl.Buffered(1)

    def mod_spec(col):
        return pl.BlockSpec((1, 1, d), lambda b, t: (jnp.where(t == 0, n_ctx_row, b), 0, col))

    in_specs = [
        pl.BlockSpec((1, tm, d), lat),
        pl.BlockSpec((1, tm, d), lambda b, t: (b, 0, 0)),
        mod_spec(0), mod_spec(1),
        pl.BlockSpec(w_p.shape, const2, pipeline_mode=once),
        pl.BlockSpec(wdec.shape, const2, pipeline_mode=once),
        pl.BlockSpec(bdec.shape, const2, pipeline_mode=once),
        pl.BlockSpec(qgain.shape, const2, pipeline_mode=once),
        pl.BlockSpec(kgain.shape, const2, pipeline_mode=once),
        pl.BlockSpec((tm, LANES), lambda b, t: (t, 0)),
        pl.BlockSpec((tm, LANES), lambda b, t: (t, 0)),
        pl.BlockSpec(trif.shape, const2, pipeline_mode=once),
        pl.BlockSpec(trib.shape, const2, pipeline_mode=once),
        pl.BlockSpec(gsum.shape, const2, pipeline_mode=once),
    ]
    out_specs = [
        pl.BlockSpec((1, tm, half_k), lat),
        pl.BlockSpec((1, tm, half_k), allr),
        pl.BlockSpec((1, tm, d), allr),
        pl.BlockSpec((1, d, tm), lambda b, t: (b, 0, t)),
        pl.BlockSpec((1, tm, half_k), allr),
        pl.BlockSpec((1, tm, half_k), allr),
        pl.BlockSpec((1, tm, d), lat),
        pl.BlockSpec((1, tm, d), lat),
        pl.BlockSpec((1, tm, d), lat),
        pl.BlockSpec((1, tm, d), lat),
        pl.BlockSpec((1, tm, d), lat),
        pl.BlockSpec((1, tm, d), allr),
        pl.BlockSpec((1, d, tm), lambda b, t: (b, 0, t)),
    ]
    sd = jax.ShapeDtypeStruct
    out_shape = [
        sd((b_, l_, half_k), F32), sd((b_, lt, half_k), F32), sd((b_, lt, d), BF16), sd((b_, d, lt), BF16),
        sd((b_, lt, half_k), F32), sd((b_, lt, half_k), F32),
        sd((b_, l_, d), BF16), sd((b_, l_, d), BF16), sd((b_, l_, d), BF16), sd((b_, l_, d), BF16),
        sd((b_, l_, d), BF16), sd((b_, lt, d), BF16), sd((b_, d, lt), BF16),
    ]
    return pl.pallas_call(
        functools.partial(_proj_kernel, offs),
        grid=(b_, nt),
        in_specs=in_specs, out_specs=out_specs, out_shape=out_shape,
        compiler_params=pltpu.CompilerParams(dimension_semantics=("arbitrary", "arbitrary"),
                                             vmem_limit_bytes=VMEM_LIMIT),
        name="proj",
    )(x, ctx, mod3, mod3, w_p, wdec, bdec, qgain, kgain, cos, sin_signed, trif, trib, gsum)


def _chunk_rows(rows, dk):
    return jnp.concatenate([jnp.broadcast_to(r, (GLA_CHUNK, dk)) for r in rows], axis=0)


def _gla_direction(forward, q_ref, k_ref, v_ref, vt_ref, c_ref, o_ref, st_ref):
    tm = k_ref.shape[1]
    nh = GLA_HEADS
    dk = k_ref.shape[2] // nh
    dv = v_ref.shape[2] // nh
    nc = tm // GLA_CHUNK
    half = GLA_CHUNK // 2
    ri = lax.broadcasted_iota(jnp.int32, (tm, tm), 0)
    ci = lax.broadcasted_iota(jnp.int32, (tm, tm), 1)
    same = (ri // GLA_CHUNK) == (ci // GLA_CHUNK)
    mask = same & ((ri >= ci) if forward else (ci >= ri))
    row_chunk = lax.broadcasted_iota(jnp.int32, (tm, dk), 0) // GLA_CHUNK
    order = list(range(nc)) if forward else list(reversed(range(nc)))
    zrow = jnp.zeros((1, dk), F32)
    for h in range(nh):
        ks = slice(h * dk, (h + 1) * dk)
        vs = slice(h * dv, (h + 1) * dv)
        cum = c_ref[0, :, ks]
        q = q_ref[0, :, ks]
        k = k_ref[0, :, ks]
        v = v_ref[0, :, vs]
        vt = vt_ref[0, vs, :]
        last, ref = [], []
        for c in range(nc):
            base = c * GLA_CHUNK
            li = base + GLA_CHUNK - 1 if forward else base
            ce = base + half - 1 if forward else base + half
            last.append(cum[li:li + 1])
            ref.append(cum[ce:ce + 1])
        lastb = _chunk_rows(last, dk)
        refb = _chunk_rows(ref, dk)
        qe = (q * jnp.exp(cum - refb)).astype(BF16)
        ke = (k * jnp.exp(refb - cum)).astype(BF16)
        a = jnp.where(mask, _dot_nt(qe, ke), 0.0).astype(BF16)
        o_intra = _dot(a, v)
        p_start, p_end, run = {}, {}, zrow
        for c in order:
            p_start[c] = run
            run = run + last[c]
            p_end[c] = run
        p_total = run
        qd = q * jnp.exp(cum)
        kd = k * jnp.exp(lastb - cum)
        sources = order[:-1]
        k_cols = [jnp.where(row_chunk == s, kd, 0.0).astype(BF16) for s in sources]
        k_cols.append((kd * jnp.exp(_chunk_rows([p_total - p_end[c] for c in range(nc)], dk))).astype(BF16))
        ut = _dot(vt, jnp.concatenate(k_cols, axis=1))
        st = st_ref[h]
        wt = jnp.concatenate([st, ut[:, :(nc - 1) * dk]], axis=1).astype(BF16)
        q_cols = [(qd * jnp.exp(_chunk_rows([p_start[c] for c in range(nc)], dk))).astype(BF16)]
        for i, s in enumerate(sources):
            later = order[i + 1:]
            fac = [jnp.exp(p_start[c] - p_end[s]) if c in later else zrow for c in range(nc)]
            q_cols.append((qd * _chunk_rows(fac, dk)).astype(BF16))
        o_inter = _dot_nt(jnp.concatenate(q_cols, axis=1), wt)
        st_ref[h] = st * jnp.exp(p_total) + ut[:, (nc - 1) * dk:]
        o_ref[0, :, vs] = (o_intra + o_inter).astype(o_ref.dtype)


def _gla_kernel(qf_ref, kf_ref, vf_ref, vtf_ref, cf_ref, qb_ref, kb_ref, vb_ref, vtb_ref, cb_ref,
                of_ref, ob_ref, sf_ref, sb_ref):
    @pl.when(pl.program_id(1) == 0)
    def _():
        sf_ref[...] = jnp.zeros_like(sf_ref)
        sb_ref[...] = jnp.zeros_like(sb_ref)

    _gla_direction(True, qf_ref, kf_ref, vf_ref, vtf_ref, cf_ref, of_ref, sf_ref)
    _gla_direction(False, qb_ref, kb_ref, vb_ref, vtb_ref, cb_ref, ob_ref, sb_ref)


def _gla_call(gq, gk, gv, gvt, cumf, cumb):
    b_, l_, kw = gq.shape
    lt = gk.shape[1]
    d = gv.shape[2]
    tm = ROW_TILE
    nt = lt // tm
    nlat = l_ // tm
    dk = kw // GLA_HEADS
    dv = d // GLA_HEADS

    bwd_blk = lambda s: jnp.where(s == 0, 0, nt - s)
    f_lat = lambda b, s: (b, jnp.maximum(s - 1, 0), 0)
    f_all = lambda b, s: (b, s, 0)
    b_lat = lambda b, s: (b, jnp.where(s == 0, nlat - 1, nlat - s), 0)
    b_all = lambda b, s: (b, bwd_blk(s), 0)
    return pl.pallas_call(
        _gla_kernel,
        grid=(b_, nt),
        in_specs=[
            pl.BlockSpec((1, tm, kw), f_lat), pl.BlockSpec((1, tm, kw), f_all),
            pl.BlockSpec((1, tm, d), f_all), pl.BlockSpec((1, d, tm), lambda b, s: (b, 0, s)),
            pl.BlockSpec((1, tm, kw), f_all),
            pl.BlockSpec((1, tm, kw), b_lat), pl.BlockSpec((1, tm, kw), b_all),
            pl.BlockSpec((1, tm, d), b_all), pl.BlockSpec((1, d, tm), lambda b, s: (b, 0, bwd_blk(s))),
            pl.BlockSpec((1, tm, kw), b_all),
        ],
        out_specs=[pl.BlockSpec((1, tm, d), f_lat), pl.BlockSpec((1, tm, d), b_lat)],
        out_shape=[jax.ShapeDtypeStruct((b_, l_, d), BF16), jax.ShapeDtypeStruct((b_, l_, d), BF16)],
        scratch_shapes=[pltpu.VMEM((GLA_HEADS, dv, dk), F32), pltpu.VMEM((GLA_HEADS, dv, dk), F32)],
        compiler_params=pltpu.CompilerParams(dimension_semantics=("arbitrary", "arbitrary"),
                                             vmem_limit_bytes=VMEM_LIMIT),
        name="gla",
    )(gq, gk, gv, gvt, cumf, gq, gk, gv, gvt, cumb)


def _attn_kernel(tk, scal_ref, q_ref, k_ref, vt_ref, o_ref, acc1, acc2, m1, l1, m2, l2):
    tq = q_ref.shape[1]
    nk = k_ref.shape[1] // tk
    q = q_ref[0]
    lane = lax.broadcasted_iota(jnp.int32, q.shape, 1)
    zero = jnp.zeros_like(q)
    q1 = jnp.where(lane < DIFF_DK, q, zero)
    q2 = jnp.where(lane >= DIFF_DK, q, zero)
    streams = ((q1, acc1, m1, l1), (q2, acc2, m2, l2))
    for r in (acc1, acc2, l1, l2):
        r[...] = jnp.zeros_like(r)

    def tiles(j):
        off = pl.multiple_of(j * tk, tk)
        return k_ref[0, pl.ds(off, tk), :], vt_ref[0, :, pl.ds(off, tk)]

    def bounded_body(j, carry):
        kk, vt = tiles(j)
        ps = [jnp.exp2(_dot_nt(kk, qq)) for qq, _, _, _ in streams]
        for p, (_, acc, _, l) in zip(ps, streams):
            l[...] += jnp.sum(p.reshape(tk // 8, 8, tq), axis=0)
            acc[...] += _dot(vt, p.astype(BF16))
        return carry

    def online_body(j, carry):
        kk, vt = tiles(j)
        for qq, acc, m, l in streams:
            s = _dot_nt(kk, qq)
            m_new = jnp.maximum(m[...], jnp.max(s, axis=0, keepdims=True))
            alpha = jnp.exp2(m[...] - m_new)
            p = jnp.exp2(s - m_new)
            l[0:1] = alpha * l[0:1] + jnp.sum(p, axis=0, keepdims=True)
            acc[...] = alpha * acc[...] + _dot(vt, p.astype(BF16))
            m[...] = m_new
        return carry

    bounded = scal_ref[1] != 0.0

    @pl.when(bounded)
    def _():
        lax.fori_loop(0, nk, bounded_body, 0, unroll=True)

    @pl.when(jnp.logical_not(bounded))
    def _():
        m1[...] = jnp.full_like(m1, -1e30)
        m2[...] = jnp.full_like(m2, -1e30)
        lax.fori_loop(0, nk, online_body, 0)

    inv1 = 1.0 / jnp.sum(l1[...], axis=0, keepdims=True)
    inv2 = 1.0 / jnp.sum(l2[...], axis=0, keepdims=True)
    o = acc1[...] * inv1 - scal_ref[0] * (acc2[...] * inv2)
    o = o * lax.rsqrt(jnp.mean(o * o, axis=0, keepdims=True) + EPS)
    o_ref[0] = o.T.astype(o_ref.dtype)


def _attn_call(scal, dq, dk, dvt, tq, tk):
    b_, l_, d = dq.shape
    lt = dk.shape[1]
    nh = d // LANES
    assert l_ % tq == 0 and lt % tk == 0
    return pl.pallas_call(
        functools.partial(_attn_kernel, tk),
        grid=(b_, nh, l_ // tq),
        in_specs=[
            pl.BlockSpec(memory_space=pltpu.SMEM),
            pl.BlockSpec((1, tq, LANES), lambda b, h, i: (b, i, h)),
            pl.BlockSpec((1, lt, LANES), lambda b, h, i: (b, 0, h)),
            pl.BlockSpec((1, LANES, lt), lambda b, h, i: (b, h, 0)),
        ],
        out_specs=pl.BlockSpec((1, tq, LANES), lambda b, h, i: (b, i, h)),
        out_shape=jax.ShapeDtypeStruct((b_, l_, d), BF16),
        scratch_shapes=[pltpu.VMEM((LANES, tq), F32), pltpu.VMEM((LANES, tq), F32),
                        pltpu.VMEM((1, tq), F32), pltpu.VMEM((8, tq), F32),
                        pltpu.VMEM((1, tq), F32), pltpu.VMEM((8, tq), F32)],
        compiler_params=pltpu.CompilerParams(
            dimension_semantics=("arbitrary", "arbitrary", "arbitrary"),
            vmem_limit_bytes=VMEM_LIMIT),
        name="diffattn",
    )(scal, dq, dk, dvt)


def _combine_kernel(x_ref, g_ref, of_ref, ob_ref, od_ref, gg_ref, dg_ref, mg_ref, md_ref,
                    ggain_ref, dgain_ref, wg_ref, wd_ref, wo_ref, out_ref):
    d = x_ref.shape[2]
    dv = d // GLA_HEADS
    og = of_ref[0].astype(F32) + ob_ref[0].astype(F32)
    normed = []
    for h in range(GLA_HEADS):
        sl = og[:, h * dv:(h + 1) * dv]
        normed.append(sl * lax.rsqrt(jnp.mean(sl * sl, axis=-1, keepdims=True) + EPS))
    a = jnp.concatenate(normed, axis=1) * ggain_ref[...] * gg_ref[0].astype(F32)
    bb = od_ref[0].astype(F32) * dgain_ref[...] * dg_ref[0].astype(F32)
    y = (mg_ref[0].astype(F32) * _dot(a.astype(BF16), wg_ref[...])
         + md_ref[0].astype(F32) * _dot(bb.astype(BF16), wd_ref[...]))
    out_ref[0] = x_ref[0] + g_ref[0] * _dot(y.astype(BF16), wo_ref[...])


def _combine_call(x, mod3, o_f, o_b, o_d, gg, dg, mg, md, ggain, dgain, wg, wd, wo, tm):
    b_, l_, d = x.shape
    row = lambda b, i: (b, i, 0)
    const2 = lambda b, i: (0, 0)
    once = pl.Buffered(1)
    big = pl.BlockSpec((1, tm, d), row)
    wspec = pl.BlockSpec((d, d), const2, pipeline_mode=once)
    return pl.pallas_call(
        _combine_kernel,
        grid=(b_, l_ // tm),
        in_specs=[big, pl.BlockSpec((1, 1, d), lambda b, i: (b, 0, 2)),
                  big, big, big, big, big, big, big,
                  pl.BlockSpec((1, d), const2, pipeline_mode=once),
                  pl.BlockSpec((1, d), const2, pipeline_mode=once),
                  wspec, wspec, wspec],
        out_specs=big,
        out_shape=jax.ShapeDtypeStruct(x.shape, x.dtype),
        compiler_params=pltpu.CompilerParams(dimension_semantics=("arbitrary", "arbitrary"),
                                             vmem_limit_bytes=VMEM_LIMIT),
        name="combine",
    )(x, mod3, o_f, o_b, o_d, gg, dg, mg, md, ggain, dgain, wg, wd, wo)


def _rope_tables(n_tokens, n_ctx):
    f32 = np.float32
    rows = n_tokens // GRID_W
    inv_freq = f32(ROPE_BASE) ** (-np.arange(0, ROPE_AXIS_DIM, 2, dtype=f32) / f32(ROPE_AXIS_DIM))
    ang_r = (np.arange(rows, dtype=f32)[:, None] * inv_freq).astype(f32)
    ang_c = (np.arange(GRID_W, dtype=f32)[:, None] * inv_freq).astype(f32)
    sign = np.where(np.arange(ROPE_AXIS_DIM) < ROPE_AXIS_DIM // 2, -1.0, 1.0)
    two = lambda t: np.concatenate([t, t], axis=-1).astype(f32)

    def table(fn, sgn):
        tr = jnp.broadcast_to(jnp.asarray(two(fn(ang_r)) * sgn, F32)[:, None, :], (rows, GRID_W, ROPE_AXIS_DIM))
        tc = jnp.broadcast_to(jnp.asarray(two(fn(ang_c)) * sgn, F32)[None, :, :], (rows, GRID_W, ROPE_AXIS_DIM))
        return jnp.concatenate([tr, tc, tr, tc], axis=-1).reshape(n_tokens, LANES)

    cos = jnp.concatenate([jnp.ones((n_ctx, LANES), F32), table(np.cos, 1.0)], axis=0)
    sin = jnp.concatenate([jnp.zeros((n_ctx, LANES), F32), table(np.sin, sign)], axis=0)
    return cos, sin


def _block_tri(n, upper):
    i = np.arange(n)
    same = (i[:, None] // GLA_CHUNK) == (i[None, :] // GLA_CHUNK)
    tri = (i[None, :] >= i[:, None]) if upper else (i[:, None] >= i[None, :])
    return jnp.asarray((same & tri).astype(np.float32)).astype(BF16)


def _attn_tiles(l_, lt):
    tq = 1024 if l_ % 1024 == 0 else l_
    for tk in (768, 512, 384, 256, 128):
        if lt % tk == 0:
            return tq, tk
    raise ValueError("key length must be a multiple of 128")


def kernel(x, c, ctx, c_ctx, w_ada, b_ada, w_in, gla_w_decay, gla_b_decay, gla_norm, diff_q_norm,
           diff_k_norm, diff_lambda, diff_norm, w_br_gla, w_br_diff, w_out):
    b_, l_, d = x.shape
    n_ctx = ctx.shape[1]
    depth = w_ada.shape[0]
    assert depth == 1, "single-layer block"
    layer = 0
    lam_init = 0.8 - 0.6 * math.exp(-0.3 * layer)
    half_k = d // 2

    cc = jnp.zeros((8, d), F32).at[:b_].set(c).at[b_].set(c_ctx)
    mod, lam_tile = _mod_call(cc, w_ada[layer], b_ada[layer][None, :], diff_lambda[layer], lam_init)
    mod3 = mod.reshape(8, 1, 3 * d)
    lam = lam_tile[0, :1]

    w = w_in[layer]
    sizes = (half_k, half_k, d, d, 2 * GLA_RANK, d, d, d, d, d, d)
    starts = np.concatenate([[0], np.cumsum(sizes)])
    col = lambda i: w[:, int(starts[i]):int(starts[i + 1])]
    names = ("gq", "gk", "gv", "gg", "dq", "dk", "dv", "dg", "mg", "md", "lr")
    order = (0, 1, 2, 3, 5, 6, 7, 8, 9, 10)
    pieces = [col(i) for i in order] + [col(4)] * 3 + [jnp.zeros((d, LANES - 6 * GLA_RANK), w.dtype)]
    w_p = jnp.concatenate(pieces, axis=1).astype(BF16)
    widths = [sizes[i] for i in order] + [LANES]
    offs, pos = {}, 0
    for name, wd_ in zip(names, widths):
        offs[name] = (pos, pos + wd_)
        pos += wd_

    zk = jnp.zeros((GLA_RANK, half_k), F32)
    wd_f32 = jnp.concatenate([jnp.concatenate([gla_w_decay[layer, 0], zk], axis=1),
                              jnp.concatenate([zk, gla_w_decay[layer, 1]], axis=1)], axis=0)
    wd_hi = wd_f32.astype(BF16)
    wd_lo = (wd_f32 - wd_hi.astype(F32)).astype(BF16)
    wdec = jnp.concatenate([wd_hi, wd_lo, wd_hi,
                            jnp.zeros((LANES - 6 * GLA_RANK, 2 * half_k), BF16)], axis=0)
    bdec = jnp.concatenate([gla_b_decay[layer, 0], gla_b_decay[layer, 1]])[None, :]
    reps = d // DIFF_DK
    qgain = (jnp.tile(diff_q_norm[layer], reps) * (DIFF_DK ** -0.5 * LOG2E))[None, :]
    kgain = jnp.tile(diff_k_norm[layer], reps)[None, :]
    cos, sin_signed = _rope_tables(l_, n_ctx)
    trif = _block_tri(ROW_TILE, upper=False)
    trib = _block_tri(ROW_TILE, upper=True)
    gi = np.arange(256)
    gsum = jnp.asarray((gi[:, None] // DIFF_DK == gi[None, :] // DIFF_DK).astype(np.float32)).astype(BF16)

    (gq, gk, gv, gvt, cumf, cumb, gg, dg, mg, md, dq, dk, dvt) = _proj_call(
        x, ctx, mod3, w_p, offs, wdec, bdec, qgain, kgain, cos, sin_signed, trif, trib, gsum)

    o_f, o_b = _gla_call(gq, gk, gv, gvt, cumf, cumb)

    tq, tk = _attn_tiles(l_, l_ + n_ctx)
    score_bound = DIFF_DK ** 0.5 * jnp.max(jnp.abs(diff_q_norm[layer])) * jnp.max(jnp.abs(diff_k_norm[layer]))
    bounded = (score_bound * BOUND_MARGIN <= SCORE_BOUND).astype(F32)
    scal = jnp.concatenate([lam, bounded[None]])
    o_d = _attn_call(scal, dq, dk, dvt, tq, tk)

    ggain = jnp.tile(gla_norm[layer], GLA_HEADS)[None, :]
    dgain = (jnp.tile(diff_norm[layer], d // diff_norm.shape[1]) * (1.0 - lam_init))[None, :]
    tm = 512 if l_ % 512 == 0 else l_
    return _combine_call(x, mod3, o_f, o_b, o_d, gg, dg, mg, md, ggain, dgain,
                         w_br_gla[layer].astype(BF16), w_br_diff[layer].astype(BF16),
                         w_out[layer].astype(BF16), tm)
```

```python
import functools
import math

import jax
import jax.numpy as jnp
import numpy as np
from jax import lax
from jax.experimental import pallas as pl
from jax.experimental.pallas import tpu as pltpu

F32 = jnp.float32
BF16 = jnp.bfloat16
HIGHEST = lax.Precision.HIGHEST

EPS = 1e-6
GRID_W = 64
GLA_HEADS = 4
GLA_RANK = 16
GLA_TAU = 16.0
GLA_CHUNK = 64
DIFF_DK = 64
ROPE_BASE = 10000.0
ROPE_AXIS_DIM = DIFF_DK // 2
LANES = 128
ROW_TILE = 256
VMEM_LIMIT = 56 * 1024 * 1024
LOG2E = math.log2(math.e)
SCORE_BOUND = 40.0
BOUND_MARGIN = 1.02


def _dot(a, b):
    return jnp.dot(a, b, preferred_element_type=F32)


def _dot_nt(a, b):
    return lax.dot_general(a, b, (((1,), (1,)), ((), ())), preferred_element_type=F32)


def _dot_exact(a, b):
    return lax.dot_general(a, b, (((1,), (0,)), ((), ())), precision=HIGHEST,
                           preferred_element_type=F32)


def _split_bf16(a):
    hi = a.astype(BF16)
    return hi, (a - hi.astype(F32)).astype(BF16)


def _sigmoid(t):
    return 0.5 * jnp.tanh(0.5 * t) + 0.5


def _silu(t):
    return t * _sigmoid(t)


def _mod_kernel(lam_init, cc_ref, w_ref, b_ref, lamp_ref, mod_ref, lam_ref):
    cc = cc_ref[...]
    mod_ref[...] = _dot_exact(_silu(cc), w_ref[...]) + b_ref[...]
    lp = lamp_ref[...]
    e1 = jnp.exp(jnp.sum(lp[0:1] * lp[1:2], axis=-1, keepdims=True))
    e2 = jnp.exp(jnp.sum(lp[2:3] * lp[3:4], axis=-1, keepdims=True))
    lam_ref[...] = jnp.broadcast_to(e1 - e2 + lam_init, lam_ref.shape)


def _mod_call(cc, w_ada, b_ada, lam_params, lam_init):
    d = cc.shape[1]
    n = w_ada.shape[1]
    return pl.pallas_call(
        functools.partial(_mod_kernel, lam_init),
        grid=(n // d,),
        in_specs=[
            pl.BlockSpec((8, d), lambda j: (0, 0)),
            pl.BlockSpec((d, d), lambda j: (0, j)),
            pl.BlockSpec((1, d), lambda j: (0, j)),
            pl.BlockSpec(lam_params.shape, lambda j: (0, 0)),
        ],
        out_specs=[
            pl.BlockSpec((8, d), lambda j: (0, j)),
            pl.BlockSpec((8, LANES), lambda j: (0, 0)),
        ],
        out_shape=[jax.ShapeDtypeStruct((8, n), F32), jax.ShapeDtypeStruct((8, LANES), F32)],
        compiler_params=pltpu.CompilerParams(dimension_semantics=("arbitrary",)),
        name="mod",
    )(cc, w_ada, b_ada, lam_params)


def _group_norm_rope(p, gsum_ref, gain_ref, cos, sin_signed, lo_mask):
    tm, d = p.shape
    sq = (p * p).astype(BF16)
    slabs = []
    for s in range(d // 256):
        slabs.append(_dot(sq[:, s * 256:(s + 1) * 256], gsum_ref[...]))
    ss = jnp.concatenate(slabs, axis=1)
    n = p * lax.rsqrt(ss * (1.0 / DIFF_DK) + EPS) * gain_ref[...]
    outs = []
    for h in range(d // LANES):
        sl = n[:, h * LANES:(h + 1) * LANES]
        up = pltpu.roll(sl, LANES - ROPE_AXIS_DIM // 2, axis=1)
        dn = pltpu.roll(sl, ROPE_AXIS_DIM // 2, axis=1)
        outs.append(sl * cos + jnp.where(lo_mask, up, dn) * sin_signed)
    return jnp.concatenate(outs, axis=1)


def _proj_kernel(offs, x_ref, ctx_ref, shift_ref, scale_ref, wa_ref, wb_ref, wlr_ref, wdec_ref, bdec_ref,
                 qgain_ref, kgain_ref, cos_ref, sin_ref, trif_ref, trib_ref, gsum_ref,
                 gq_ref, gk_ref, gv_ref, gvt_ref, cf_ref, cb_ref, gg_ref, dg_ref, mg_ref, md_ref,
                 dq_ref, dk_ref, dvt_ref):
    t = pl.program_id(1)
    latent = t > 0
    xin = jnp.where(latent, x_ref[0], ctx_ref[0])
    ms = jnp.mean(xin * xin, axis=-1, keepdims=True)
    h = xin * lax.rsqrt(ms + EPS) * (1.0 + scale_ref[0]) + shift_ref[0]
    hb = h.astype(BF16)

    def proj(name):
        which, a, b = offs[name]
        return _dot(hb, (wa_ref, wb_ref, wlr_ref)[which][:, a:b])

    lr3 = proj("lr")
    lr_hi, lr_lo = _split_bf16(lr3)
    lane = lax.broadcasted_iota(jnp.int32, lr3.shape, 1)
    z = _dot(jnp.where(lane < 4 * GLA_RANK, lr_hi, lr_lo), wdec_ref[...]) + bdec_ref[...]
    la = (jnp.minimum(z, 0.0) - jnp.log(1.0 + jnp.exp(-jnp.abs(z)))) * (1.0 / GLA_TAU)
    half = la.shape[1] // 2
    la_hi, la_lo = _split_bf16(la)

    dk_gla = gq_ref.shape[2] // GLA_HEADS
    gq_ref[0] = proj("gq") * dk_gla ** -0.5
    gk_ref[0] = proj("gk")
    gv = proj("gv")
    gv_ref[0] = gv.astype(BF16)
    gvt_ref[0] = gv.T.astype(BF16)

    gg_ref[0] = _silu(proj("gg")).astype(BF16)
    dg_ref[0] = _silu(proj("dg")).astype(BF16)
    mg_ref[0] = _sigmoid(proj("mg")).astype(BF16)
    md_ref[0] = _sigmoid(proj("md")).astype(BF16)

    cf_ref[0] = _dot(trif_ref[...], la_hi[:, :half]) + _dot(trif_ref[...], la_lo[:, :half])
    cb_ref[0] = _dot(trib_ref[...], la_hi[:, half:]) + _dot(trib_ref[...], la_lo[:, half:])

    cos = cos_ref[...]
    sin_signed = sin_ref[...]
    lane = lax.broadcasted_iota(jnp.int32, cos.shape, 1)
    lo_mask = (lane % ROPE_AXIS_DIM) < (ROPE_AXIS_DIM // 2)
    dq_ref[0] = _group_norm_rope(proj("dq"), gsum_ref, qgain_ref, cos, sin_signed, lo_mask).astype(BF16)
    dk_ref[0] = _group_norm_rope(proj("dk"), gsum_ref, kgain_ref, cos, sin_signed, lo_mask).astype(BF16)
    dvt_ref[0] = proj("dv").T.astype(BF16)


def _proj_call(x, ctx, mod3, w_a, w_b, w_lr, offs, wdec, bdec, qgain, kgain, cos, sin_signed,
               trif, trib, gsum):
    b_, l_, d = x.shape
    tm = ROW_TILE
    assert ctx.shape[1] == tm and l_ % tm == 0
    nt = l_ // tm + 1
    lt = l_ + tm
    n_ctx_row = b_
    half_k = offs["gq"][2] - offs["gq"][1]

    lat = lambda b, t: (b, jnp.maximum(t - 1, 0), 0)
    allr = lambda b, t: (b, t, 0)
    const2 = lambda b, t: (0, 0)
    once = pl.Buffered(1)

    def mod_spec(col):
        return pl.BlockSpec((1, 1, d), lambda b, t: (jnp.where(t == 0, n_ctx_row, b), 0, col))

    in_specs = [
        pl.BlockSpec((1, tm, d), lat),
        pl.BlockSpec((1, tm, d), lambda b, t: (b, 0, 0)),
        mod_spec(0), mod_spec(1),
        pl.BlockSpec(w_a.shape, const2, pipeline_mode=once),
        pl.BlockSpec(w_b.shape, const2, pipeline_mode=once),
        pl.BlockSpec(w_lr.shape, const2, pipeline_mode=once),
        pl.BlockSpec(wdec.shape, const2, pipeline_mode=once),
        pl.BlockSpec(bdec.shape, const2, pipeline_mode=once),
        pl.BlockSpec(qgain.shape, const2, pipeline_mode=once),
        pl.BlockSpec(kgain.shape, const2, pipeline_mode=once),
        pl.BlockSpec((tm, LANES), lambda b, t: (t, 0)),
        pl.BlockSpec((tm, LANES), lambda b, t: (t, 0)),
        pl.BlockSpec(trif.shape, const2, pipeline_mode=once),
        pl.BlockSpec(trib.shape, const2, pipeline_mode=once),
        pl.BlockSpec(gsum.shape, const2, pipeline_mode=once),
    ]
    out_specs = [
        pl.BlockSpec((1, tm, half_k), lat),
        pl.BlockSpec((1, tm, half_k), allr),
        pl.BlockSpec((1, tm, d), allr),
        pl.BlockSpec((1, d, tm), lambda b, t: (b, 0, t)),
        pl.BlockSpec((1, tm, half_k), allr),
        pl.BlockSpec((1, tm, half_k), allr),
        pl.BlockSpec((1, tm, d), lat),
        pl.BlockSpec((1, tm, d), lat),
        pl.BlockSpec((1, tm, d), lat),
        pl.BlockSpec((1, tm, d), lat),
        pl.BlockSpec((1, tm, d), lat),
        pl.BlockSpec((1, tm, d), allr),
        pl.BlockSpec((1, d, tm), lambda b, t: (b, 0, t)),
    ]
    sd = jax.ShapeDtypeStruct
    out_shape = [
        sd((b_, l_, half_k), F32), sd((b_, lt, half_k), F32), sd((b_, lt, d), BF16), sd((b_, d, lt), BF16),
        sd((b_, lt, half_k), F32), sd((b_, lt, half_k), F32),
        sd((b_, l_, d), BF16), sd((b_, l_, d), BF16), sd((b_, l_, d), BF16), sd((b_, l_, d), BF16),
        sd((b_, l_, d), BF16), sd((b_, lt, d), BF16), sd((b_, d, lt), BF16),
    ]
    return pl.pallas_call(
        functools.partial(_proj_kernel, offs),
        grid=(b_, nt),
        in_specs=in_specs, out_specs=out_specs, out_shape=out_shape,
        compiler_params=pltpu.CompilerParams(dimension_semantics=("arbitrary", "arbitrary"),
                                             vmem_limit_bytes=VMEM_LIMIT),
        name="proj",
    )(x, ctx, mod3, mod3, w_a, w_b, w_lr, wdec, bdec, qgain, kgain, cos, sin_signed, trif, trib, gsum)


def _chunk_rows(rows, dk):
    return jnp.concatenate([jnp.broadcast_to(r, (GLA_CHUNK, dk)) for r in rows], axis=0)


def _gla_direction(forward, q_ref, k_ref, v_ref, vt_ref, c_ref, o_ref, st_ref):
    tm = k_ref.shape[1]
    nh = GLA_HEADS
    dk = k_ref.shape[2] // nh
    dv = v_ref.shape[2] // nh
    nc = tm // GLA_CHUNK
    half = GLA_CHUNK // 2
    ri = lax.broadcasted_iota(jnp.int32, (tm, tm), 0)
    ci = lax.broadcasted_iota(jnp.int32, (tm, tm), 1)
    same = (ri // GLA_CHUNK) == (ci // GLA_CHUNK)
    mask = same & ((ri >= ci) if forward else (ci >= ri))
    row_chunk = lax.broadcasted_iota(jnp.int32, (tm, dk), 0) // GLA_CHUNK
    order = list(range(nc)) if forward else list(reversed(range(nc)))
    zrow = jnp.zeros((1, dk), F32)
    for h in range(nh):
        ks = slice(h * dk, (h + 1) * dk)
        vs = slice(h * dv, (h + 1) * dv)
        cum = c_ref[0, :, ks]
        q = q_ref[0, :, ks]
        k = k_ref[0, :, ks]
        v = v_ref[0, :, vs]
        vt = vt_ref[0, vs, :]
        last, ref = [], []
        for c in range(nc):
            base = c * GLA_CHUNK
            li = base + GLA_CHUNK - 1 if forward else base
            ce = base + half - 1 if forward else base + half
            last.append(cum[li:li + 1])
            ref.append(cum[ce:ce + 1])
        lastb = _chunk_rows(last, dk)
        refb = _chunk_rows(ref, dk)
        qe = (q * jnp.exp(cum - refb)).astype(BF16)
        ke = (k * jnp.exp(refb - cum)).astype(BF16)
        a = jnp.where(mask, _dot_nt(qe, ke), 0.0).astype(BF16)
        o_intra = _dot(a, v)
        p_start, p_end, run = {}, {}, zrow
        for c in order:
            p_start[c] = run
            run = run + last[c]
            p_end[c] = run
        p_total = run
        qd = q * jnp.exp(cum)
        kd = k * jnp.exp(lastb - cum)
        sources = order[:-1]
        k_cols = [jnp.where(row_chunk == s, kd, 0.0).astype(BF16) for s in sources]
        k_cols.append((kd * jnp.exp(_chunk_rows([p_total - p_end[c] for c in range(nc)], dk))).astype(BF16))
        ut = _dot(vt, jnp.concatenate(k_cols, axis=1))
        st = st_ref[h]
        wt = jnp.concatenate([st, ut[:, :(nc - 1) * dk]], axis=1).astype(BF16)
        q_cols = [(qd * jnp.exp(_chunk_rows([p_start[c] for c in range(nc)], dk))).astype(BF16)]
        for i, s in enumerate(sources):
            later = order[i + 1:]
            fac = [jnp.exp(p_start[c] - p_end[s]) if c in later else zrow for c in range(nc)]
            q_cols.append((qd * _chunk_rows(fac, dk)).astype(BF16))
        o_inter = _dot_nt(jnp.concatenate(q_cols, axis=1), wt)
        st_ref[h] = st * jnp.exp(p_total) + ut[:, (nc - 1) * dk:]
        o_ref[0, :, vs] = (o_intra + o_inter).astype(o_ref.dtype)


def _gla_kernel(qf_ref, kf_ref, vf_ref, vtf_ref, cf_ref, qb_ref, kb_ref, vb_ref, vtb_ref, cb_ref,
                of_ref, ob_ref, sf_ref, sb_ref):
    @pl.when(pl.program_id(1) == 0)
    def _():
        sf_ref[...] = jnp.zeros_like(sf_ref)
        sb_ref[...] = jnp.zeros_like(sb_ref)

    _gla_direction(True, qf_ref, kf_ref, vf_ref, vtf_ref, cf_ref, of_ref, sf_ref)
    _gla_direction(False, qb_ref, kb_ref, vb_ref, vtb_ref, cb_ref, ob_ref, sb_ref)


def _gla_call(gq, gk, gv, gvt, cumf, cumb):
    b_, l_, kw = gq.shape
    lt = gk.shape[1]
    d = gv.shape[2]
    tm = ROW_TILE
    nt = lt // tm
    nlat = l_ // tm
    dk = kw // GLA_HEADS
    dv = d // GLA_HEADS

    bwd_blk = lambda s: jnp.where(s == 0, 0, nt - s)
    f_lat = lambda b, s: (b, jnp.maximum(s - 1, 0), 0)
    f_all = lambda b, s: (b, s, 0)
    b_lat = lambda b, s: (b, jnp.where(s == 0, nlat - 1, nlat - s), 0)
    b_all = lambda b, s: (b, bwd_blk(s), 0)
    return pl.pallas_call(
        _gla_kernel,
        grid=(b_, nt),
        in_specs=[
            pl.BlockSpec((1, tm, kw), f_lat), pl.BlockSpec((1, tm, kw), f_all),
            pl.BlockSpec((1, tm, d), f_all), pl.BlockSpec((1, d, tm), lambda b, s: (b, 0, s)),
            pl.BlockSpec((1, tm, kw), f_all),
            pl.BlockSpec((1, tm, kw), b_lat), pl.BlockSpec((1, tm, kw), b_all),
            pl.BlockSpec((1, tm, d), b_all), pl.BlockSpec((1, d, tm), lambda b, s: (b, 0, bwd_blk(s))),
            pl.BlockSpec((1, tm, kw), b_all),
        ],
        out_specs=[pl.BlockSpec((1, tm, d), f_lat), pl.BlockSpec((1, tm, d), b_lat)],
        out_shape=[jax.ShapeDtypeStruct((b_, l_, d), BF16), jax.ShapeDtypeStruct((b_, l_, d), BF16)],
        scratch_shapes=[pltpu.VMEM((GLA_HEADS, dv, dk), F32), pltpu.VMEM((GLA_HEADS, dv, dk), F32)],
        compiler_params=pltpu.CompilerParams(dimension_semantics=("arbitrary", "arbitrary"),
                                             vmem_limit_bytes=VMEM_LIMIT),
        name="gla",
    )(gq, gk, gv, gvt, cumf, gq, gk, gv, gvt, cumb)


def _attn_kernel(tk, scal_ref, q_ref, k_ref, vt_ref, o_ref, acc1, acc2, m1, l1, m2, l2):
    nsub = acc1.shape[0]
    tq = q_ref.shape[1] // nsub
    nk = k_ref.shape[1] // tk
    lam = scal_ref[0]

    def streams(sub):
        q = q_ref[0, sub * tq:(sub + 1) * tq, :]
        lane = lax.broadcasted_iota(jnp.int32, q.shape, 1)
        zero = jnp.zeros_like(q)
        q1 = jnp.where(lane < DIFF_DK, q, zero)
        q2 = jnp.where(lane >= DIFF_DK, q, zero)
        return ((q1, acc1.at[sub], m1.at[sub], l1.at[sub]), (q2, acc2.at[sub], m2.at[sub], l2.at[sub]))

    def finish(sub):
        inv1 = 1.0 / jnp.sum(l1[sub], axis=0, keepdims=True)
        inv2 = 1.0 / jnp.sum(l2[sub], axis=0, keepdims=True)
        o = acc1[sub] * inv1 - lam * (acc2[sub] * inv2)
        o = o * lax.rsqrt(jnp.mean(o * o, axis=0, keepdims=True) + EPS)
        o_ref[0, sub * tq:(sub + 1) * tq, :] = o.T.astype(o_ref.dtype)

    def bounded_tiles(sub):
        st = streams(sub)

        def scores(j):
            kk = k_ref[0, j * tk:(j + 1) * tk, :]
            return [_dot_nt(kk, qq) for qq, _, _, _ in st]

        s_next = scores(0)
        for j in range(nk):
            s_cur = s_next
            if j + 1 < nk:
                s_next = scores(j + 1)
            vt = vt_ref[0, :, j * tk:(j + 1) * tk]
            for s, (_, acc, _, l) in zip(s_cur, st):
                p = jnp.exp2(s)
                l_part = jnp.sum(p.reshape(tk // 8, 8, tq), axis=0)
                pv = _dot(vt, p.astype(BF16))
                l[...] = l_part if j == 0 else l[...] + l_part
                acc[...] = pv if j == 0 else acc[...] + pv

    def online_tiles(sub):
        st = streams(sub)
        for _, acc, m, l in st:
            acc[...] = jnp.zeros_like(acc)
            l[...] = jnp.zeros_like(l)
            m[...] = jnp.full_like(m, -1e30)

        def body(j, carry):
            off = pl.multiple_of(j * tk, tk)
            kk = k_ref[0, pl.ds(off, tk), :]
            vt = vt_ref[0, :, pl.ds(off, tk)]
            for qq, acc, m, l in st:
                s = _dot_nt(kk, qq)
                m_new = jnp.maximum(m[...], jnp.max(s, axis=0, keepdims=True))
                alpha = jnp.exp2(m[...] - m_new)
                p = jnp.exp2(s - m_new)
                l[0:1] = alpha * l[0:1] + jnp.sum(p, axis=0, keepdims=True)
                acc[...] = alpha * acc[...] + _dot(vt, p.astype(BF16))
                m[...] = m_new
            return carry

        lax.fori_loop(0, nk, body, 0)

    bounded = scal_ref[1] != 0.0

    @pl.when(bounded)
    def _():
        for sub in range(nsub):
            bounded_tiles(sub)
            finish(sub)

    @pl.when(jnp.logical_not(bounded))
    def _():
        for sub in range(nsub):
            online_tiles(sub)
            finish(sub)


def _attn_call(scal, dq, dk, dvt, tq, tk, nsub):
    b_, l_, d = dq.shape
    lt = dk.shape[1]
    nh = d // LANES
    tstep = tq * nsub
    assert l_ % tstep == 0 and lt % tk == 0
    return pl.pallas_call(
        functools.partial(_attn_kernel, tk),
        grid=(b_, nh, l_ // tstep),
        in_specs=[
            pl.BlockSpec(memory_space=pltpu.SMEM),
            pl.BlockSpec((1, tstep, LANES), lambda b, h, i: (b, i, h)),
            pl.BlockSpec((1, lt, LANES), lambda b, h, i: (b, 0, h)),
            pl.BlockSpec((1, LANES, lt), lambda b, h, i: (b, h, 0)),
        ],
        out_specs=pl.BlockSpec((1, tstep, LANES), lambda b, h, i: (b, i, h)),
        out_shape=jax.ShapeDtypeStruct((b_, l_, d), BF16),
        scratch_shapes=[pltpu.VMEM((nsub, LANES, tq), F32), pltpu.VMEM((nsub, LANES, tq), F32),
                        pltpu.VMEM((nsub, 1, tq), F32), pltpu.VMEM((nsub, 8, tq), F32),
                        pltpu.VMEM((nsub, 1, tq), F32), pltpu.VMEM((nsub, 8, tq), F32)],
        compiler_params=pltpu.CompilerParams(
            dimension_semantics=("arbitrary", "arbitrary", "arbitrary"),
            vmem_limit_bytes=VMEM_LIMIT),
        name="diffattn",
    )(scal, dq, dk, dvt)


def _combine_kernel(x_ref, g_ref, of_ref, ob_ref, od_ref, gg_ref, dg_ref, mg_ref, md_ref,
                    ggain_ref, dgain_ref, wg_ref, wd_ref, wo_ref, out_ref):
    d = x_ref.shape[2]
    dv = d // GLA_HEADS
    og = of_ref[0].astype(F32) + ob_ref[0].astype(F32)
    normed = []
    for h in range(GLA_HEADS):
        sl = og[:, h * dv:(h + 1) * dv]
        normed.append(sl * lax.rsqrt(jnp.mean(sl * sl, axis=-1, keepdims=True) + EPS))
    a = jnp.concatenate(normed, axis=1) * ggain_ref[...] * gg_ref[0].astype(F32)
    bb = od_ref[0].astype(F32) * dgain_ref[...] * dg_ref[0].astype(F32)
    y = (mg_ref[0].astype(F32) * _dot(a.astype(BF16), wg_ref[...])
         + md_ref[0].astype(F32) * _dot(bb.astype(BF16), wd_ref[...]))
    out_ref[0] = x_ref[0] + g_ref[0] * _dot(y.astype(BF16), wo_ref[...])


def _combine_call(x, mod3, o_f, o_b, o_d, gg, dg, mg, md, ggain, dgain, wg, wd, wo, tm):
    b_, l_, d = x.shape
    row = lambda b, i: (b, i, 0)
    const2 = lambda b, i: (0, 0)
    once = pl.Buffered(1)
    big = pl.BlockSpec((1, tm, d), row)
    wspec = pl.BlockSpec((d, d), const2, pipeline_mode=once)
    return pl.pallas_call(
        _combine_kernel,
        grid=(b_, l_ // tm),
        in_specs=[big, pl.BlockSpec((1, 1, d), lambda b, i: (b, 0, 2)),
                  big, big, big, big, big, big, big,
                  pl.BlockSpec((1, d), const2, pipeline_mode=once),
                  pl.BlockSpec((1, d), const2, pipeline_mode=once),
                  wspec, wspec, wspec],
        out_specs=big,
        out_shape=jax.ShapeDtypeStruct(x.shape, x.dtype),
        compiler_params=pltpu.CompilerParams(dimension_semantics=("arbitrary", "arbitrary"),
                                             vmem_limit_bytes=VMEM_LIMIT),
        name="combine",
    )(x, mod3, o_f, o_b, o_d, gg, dg, mg, md, ggain, dgain, wg, wd, wo)


def _rope_tables(n_tokens, n_ctx):
    f32 = np.float32
    rows = n_tokens // GRID_W
    inv_freq = f32(ROPE_BASE) ** (-np.arange(0, ROPE_AXIS_DIM, 2, dtype=f32) / f32(ROPE_AXIS_DIM))
    ang_r = (np.arange(rows, dtype=f32)[:, None] * inv_freq).astype(f32)
    ang_c = (np.arange(GRID_W, dtype=f32)[:, None] * inv_freq).astype(f32)
    sign = np.where(np.arange(ROPE_AXIS_DIM) < ROPE_AXIS_DIM // 2, -1.0, 1.0)
    two = lambda t: np.concatenate([t, t], axis=-1).astype(f32)

    def table(fn, sgn):
        tr = jnp.broadcast_to(jnp.asarray(two(fn(ang_r)) * sgn, F32)[:, None, :], (rows, GRID_W, ROPE_AXIS_DIM))
        tc = jnp.broadcast_to(jnp.asarray(two(fn(ang_c)) * sgn, F32)[None, :, :], (rows, GRID_W, ROPE_AXIS_DIM))
        return jnp.concatenate([tr, tc, tr, tc], axis=-1).reshape(n_tokens, LANES)

    cos = jnp.concatenate([jnp.ones((n_ctx, LANES), F32), table(np.cos, 1.0)], axis=0)
    sin = jnp.concatenate([jnp.zeros((n_ctx, LANES), F32), table(np.sin, sign)], axis=0)
    return cos, sin


def _block_tri(n, upper):
    i = np.arange(n)
    same = (i[:, None] // GLA_CHUNK) == (i[None, :] // GLA_CHUNK)
    tri = (i[None, :] >= i[:, None]) if upper else (i[:, None] >= i[None, :])
    return jnp.asarray((same & tri).astype(np.float32)).astype(BF16)


def _attn_tiles(l_, lt):
    tq = 1024 if l_ % 1024 == 0 else l_
    nsub = 2 if l_ % (2 * tq) == 0 else 1
    for tk in (768, 512, 384, 256, 128):
        if lt % tk == 0:
            return tq, tk, nsub
    raise ValueError("key length must be a multiple of 128")


def kernel(x, c, ctx, c_ctx, w_ada, b_ada, w_in, gla_w_decay, gla_b_decay, gla_norm, diff_q_norm,
           diff_k_norm, diff_lambda, diff_norm, w_br_gla, w_br_diff, w_out):
    b_, l_, d = x.shape
    n_ctx = ctx.shape[1]
    depth = w_ada.shape[0]
    assert depth == 1, "single-layer block"
    layer = 0
    lam_init = 0.8 - 0.6 * math.exp(-0.3 * layer)
    half_k = d // 2

    cc = jnp.zeros((8, d), F32).at[:b_].set(c).at[b_].set(c_ctx)
    mod, lam_tile = _mod_call(cc, w_ada[layer], b_ada[layer][None, :], diff_lambda[layer], lam_init)
    mod3 = mod.reshape(8, 1, 3 * d)
    lam = lam_tile[0, :1]

    w = w_in[layer]
    sizes = (half_k, half_k, d, d, 2 * GLA_RANK, d, d, d, d, d, d)
    names = ("gq", "gk", "gv", "gg", "lr", "dq", "dk", "dv", "dg", "mg", "md")
    starts = np.concatenate([[0], np.cumsum(sizes)])
    lr_at = names.index("lr")
    lr_lo, lr_hi = int(starts[lr_at]), int(starts[lr_at + 1])
    w_a = w[:, :lr_lo].astype(BF16)
    w_b = w[:, lr_hi:].astype(BF16)
    w_lr = jnp.concatenate([w[:, lr_lo:lr_hi]] * 3 + [jnp.zeros((d, LANES - 6 * GLA_RANK), w.dtype)],
                           axis=1).astype(BF16)
    offs = {"lr": (2, 0, LANES)}
    for i, name in enumerate(names):
        if i < lr_at:
            offs[name] = (0, int(starts[i]), int(starts[i + 1]))
        elif i > lr_at:
            offs[name] = (1, int(starts[i]) - lr_hi, int(starts[i + 1]) - lr_hi)

    zk = jnp.zeros((GLA_RANK, half_k), F32)
    wd_f32 = jnp.concatenate([jnp.concatenate([gla_w_decay[layer, 0], zk], axis=1),
                              jnp.concatenate([zk, gla_w_decay[layer, 1]], axis=1)], axis=0)
    wd_hi = wd_f32.astype(BF16)
    wd_lo = (wd_f32 - wd_hi.astype(F32)).astype(BF16)
    wdec = jnp.concatenate([wd_hi, wd_lo, wd_hi,
                            jnp.zeros((LANES - 6 * GLA_RANK, 2 * half_k), BF16)], axis=0)
    bdec = jnp.concatenate([gla_b_decay[layer, 0], gla_b_decay[layer, 1]])[None, :]
    reps = d // DIFF_DK
    qgain = (jnp.tile(diff_q_norm[layer], reps) * (DIFF_DK ** -0.5 * LOG2E))[None, :]
    kgain = jnp.tile(diff_k_norm[layer], reps)[None, :]
    cos, sin_signed = _rope_tables(l_, n_ctx)
    trif = _block_tri(ROW_TILE, upper=False)
    trib = _block_tri(ROW_TILE, upper=True)
    gi = np.arange(256)
    gsum = jnp.asarray((gi[:, None] // DIFF_DK == gi[None, :] // DIFF_DK).astype(np.float32)).astype(BF16)

    (gq, gk, gv, gvt, cumf, cumb, gg, dg, mg, md, dq, dk, dvt) = _proj_call(
        x, ctx, mod3, w_a, w_b, w_lr, offs, wdec, bdec, qgain, kgain, cos, sin_signed, trif, trib, gsum)

    o_f, o_b = _gla_call(gq, gk, gv, gvt, cumf, cumb)

    tq, tk, nsub = _attn_tiles(l_, l_ + n_ctx)
    score_bound = DIFF_DK ** 0.5 * jnp.max(jnp.abs(diff_q_norm[layer])) * jnp.max(jnp.abs(diff_k_norm[layer]))
    bounded = (score_bound * BOUND_MARGIN <= SCORE_BOUND).astype(F32)
    scal = jnp.concatenate([lam, bounded[None]])
    o_d = _attn_call(scal, dq, dk, dvt, tq, tk, nsub)

    ggain = jnp.tile(gla_norm[layer], GLA_HEADS)[None, :]
    dgain = (jnp.tile(diff_norm[layer], d // diff_norm.shape[1]) * (1.0 - lam_init))[None, :]
    tm = 512 if l_ % 512 == 0 else l_
    return _combine_call(x, mod3, o_f, o_b, o_d, gg, dg, mg, md, ggain, dgain,
                         w_br_gla[layer].astype(BF16), w_br_diff[layer].astype(BF16),
                         w_out[layer].astype(BF16), tm)
```

```python
import functools
import math

import jax
import jax.numpy as jnp
import numpy as np
from jax import lax
from jax.experimental import pallas as pl
from jax.experimental.pallas import tpu as pltpu

F32 = jnp.float32
BF16 = jnp.bfloat16
HIGHEST = lax.Precision.HIGHEST

EPS = 1e-6
GRID_W = 64
GLA_HEADS = 4
GLA_RANK = 16
GLA_TAU = 16.0
GLA_CHUNK = 64
DIFF_DK = 64
ROPE_BASE = 10000.0
ROPE_AXIS_DIM = DIFF_DK // 2
LANES = 128
ROW_TILE = 256
VMEM_LIMIT = 56 * 1024 * 1024
LOG2E = math.log2(math.e)
SCORE_BOUND = 40.0
BOUND_MARGIN = 1.02


def _dot(a, b):
    return jnp.dot(a, b, preferred_element_type=F32)


def _dot_nt(a, b):
    return lax.dot_general(a, b, (((1,), (1,)), ((), ())), preferred_element_type=F32)


def _dot_exact(a, b):
    return lax.dot_general(a, b, (((1,), (0,)), ((), ())), precision=HIGHEST,
                           preferred_element_type=F32)


def _split_bf16(a):
    hi = a.astype(BF16)
    return hi, (a - hi.astype(F32)).astype(BF16)


def _sigmoid(t):
    return 0.5 * jnp.tanh(0.5 * t) + 0.5


def _silu(t):
    return t * _sigmoid(t)


def _mod_kernel(lam_init, cc_ref, w_ref, b_ref, lamp_ref, mod_ref, lam_ref):
    cc = cc_ref[...]
    mod_ref[...] = _dot_exact(_silu(cc), w_ref[...]) + b_ref[...]
    lp = lamp_ref[...]
    e1 = jnp.exp(jnp.sum(lp[0:1] * lp[1:2], axis=-1, keepdims=True))
    e2 = jnp.exp(jnp.sum(lp[2:3] * lp[3:4], axis=-1, keepdims=True))
    lam_ref[...] = jnp.broadcast_to(e1 - e2 + lam_init, lam_ref.shape)


def _mod_call(cc, w_ada, b_ada, lam_params, lam_init):
    d = cc.shape[1]
    n = w_ada.shape[1]
    return pl.pallas_call(
        functools.partial(_mod_kernel, lam_init),
        grid=(n // d,),
        in_specs=[
            pl.BlockSpec((8, d), lambda j: (0, 0)),
            pl.BlockSpec((d, d), lambda j: (0, j)),
            pl.BlockSpec((1, d), lambda j: (0, j)),
            pl.BlockSpec(lam_params.shape, lambda j: (0, 0)),
        ],
        out_specs=[
            pl.BlockSpec((8, d), lambda j: (0, j)),
            pl.BlockSpec((8, LANES), lambda j: (0, 0)),
        ],
        out_shape=[jax.ShapeDtypeStruct((8, n), F32), jax.ShapeDtypeStruct((8, LANES), F32)],
        compiler_params=pltpu.CompilerParams(dimension_semantics=("arbitrary",)),
        name="mod",
    )(cc, w_ada, b_ada, lam_params)


def _group_norm_rope(p, gsum_ref, gain_ref, cos, sin_signed, lo_mask):
    tm, d = p.shape
    sq = (p * p).astype(BF16)
    slabs = []
    for s in range(d // 256):
        slabs.append(_dot(sq[:, s * 256:(s + 1) * 256], gsum_ref[...]))
    ss = jnp.concatenate(slabs, axis=1)
    n = p * lax.rsqrt(ss * (1.0 / DIFF_DK) + EPS) * gain_ref[...]
    outs = []
    for h in range(d // LANES):
        sl = n[:, h * LANES:(h + 1) * LANES]
        up = pltpu.roll(sl, LANES - ROPE_AXIS_DIM // 2, axis=1)
        dn = pltpu.roll(sl, ROPE_AXIS_DIM // 2, axis=1)
        outs.append(sl * cos + jnp.where(lo_mask, up, dn) * sin_signed)
    return jnp.concatenate(outs, axis=1)


def _proj_kernel(offs, x_ref, ctx_ref, shift_ref, scale_ref, wa_ref, wb_ref, wlr_ref, wdec_ref, bdec_ref,
                 qgain_ref, kgain_ref, cos_ref, sin_ref, trif_ref, trib_ref, gsum_ref,
                 gq_ref, gk_ref, gvt_ref, cf_ref, cb_ref, gg_ref, dg_ref, mg_ref, md_ref,
                 dq_ref, dk_ref, dvt_ref):
    t = pl.program_id(1)
    latent = t > 0
    xin = jnp.where(latent, x_ref[0], ctx_ref[0])
    ms = jnp.mean(xin * xin, axis=-1, keepdims=True)
    h = xin * lax.rsqrt(ms + EPS) * (1.0 + scale_ref[0]) + shift_ref[0]
    hb = h.astype(BF16)

    def proj(name):
        which, a, b = offs[name]
        return _dot(hb, (wa_ref, wb_ref, wlr_ref)[which][:, a:b])

    lr3 = proj("lr")
    lr_hi, lr_lo = _split_bf16(lr3)
    lane = lax.broadcasted_iota(jnp.int32, lr3.shape, 1)
    z = _dot(jnp.where(lane < 4 * GLA_RANK, lr_hi, lr_lo), wdec_ref[...]) + bdec_ref[...]
    la = (jnp.minimum(z, 0.0) - jnp.log(1.0 + jnp.exp(-jnp.abs(z)))) * (1.0 / GLA_TAU)
    half = la.shape[1] // 2
    la_hi, la_lo = _split_bf16(la)

    dk_gla = gq_ref.shape[2] // GLA_HEADS
    gq_ref[0] = (proj("gq") * dk_gla ** -0.5).astype(BF16)
    gk_ref[0] = proj("gk").astype(BF16)
    gvt_ref[0] = proj("gv").T.astype(BF16)

    gg_ref[0] = _silu(proj("gg")).astype(BF16)
    dg_ref[0] = _silu(proj("dg")).astype(BF16)
    mg_ref[0] = _sigmoid(proj("mg")).astype(BF16)
    md_ref[0] = _sigmoid(proj("md")).astype(BF16)

    cf_ref[0] = _dot(trif_ref[...], la_hi[:, :half]) + _dot(trif_ref[...], la_lo[:, :half])
    cb_ref[0] = _dot(trib_ref[...], la_hi[:, half:]) + _dot(trib_ref[...], la_lo[:, half:])

    cos = cos_ref[...]
    sin_signed = sin_ref[...]
    lane = lax.broadcasted_iota(jnp.int32, cos.shape, 1)
    lo_mask = (lane % ROPE_AXIS_DIM) < (ROPE_AXIS_DIM // 2)
    dq_ref[0] = _group_norm_rope(proj("dq"), gsum_ref, qgain_ref, cos, sin_signed, lo_mask).astype(BF16)
    dk_ref[0] = _group_norm_rope(proj("dk"), gsum_ref, kgain_ref, cos, sin_signed, lo_mask).astype(BF16)
    dvt_ref[0] = proj("dv").T.astype(BF16)


def _proj_call(x, ctx, mod3, w_a, w_b, w_lr, offs, wdec, bdec, qgain, kgain, cos, sin_signed,
               trif, trib, gsum):
    b_, l_, d = x.shape
    tm = ROW_TILE
    assert ctx.shape[1] == tm and l_ % tm == 0
    nt = l_ // tm + 1
    lt = l_ + tm
    n_ctx_row = b_
    half_k = offs["gq"][2] - offs["gq"][1]

    lat = lambda b, t: (b, jnp.maximum(t - 1, 0), 0)
    allr = lambda b, t: (b, t, 0)
    const2 = lambda b, t: (0, 0)
    once = pl.Buffered(1)

    def mod_spec(col):
        return pl.BlockSpec((1, 1, d), lambda b, t: (jnp.where(t == 0, n_ctx_row, b), 0, col))

    in_specs = [
        pl.BlockSpec((1, tm, d), lat),
        pl.BlockSpec((1, tm, d), lambda b, t: (b, 0, 0)),
        mod_spec(0), mod_spec(1),
        pl.BlockSpec(w_a.shape, const2, pipeline_mode=once),
        pl.BlockSpec(w_b.shape, const2, pipeline_mode=once),
        pl.BlockSpec(w_lr.shape, const2, pipeline_mode=once),
        pl.BlockSpec(wdec.shape, const2, pipeline_mode=once),
        pl.BlockSpec(bdec.shape, const2, pipeline_mode=once),
        pl.BlockSpec(qgain.shape, const2, pipeline_mode=once),
        pl.BlockSpec(kgain.shape, const2, pipeline_mode=once),
        pl.BlockSpec((tm, LANES), lambda b, t: (t, 0)),
        pl.BlockSpec((tm, LANES), lambda b, t: (t, 0)),
        pl.BlockSpec(trif.shape, const2, pipeline_mode=once),
        pl.BlockSpec(trib.shape, const2, pipeline_mode=once),
        pl.BlockSpec(gsum.shape, const2, pipeline_mode=once),
    ]
    out_specs = [
        pl.BlockSpec((1, tm, half_k), lat),
        pl.BlockSpec((1, tm, half_k), allr),
        pl.BlockSpec((1, d, tm), lambda b, t: (b, 0, t)),
        pl.BlockSpec((1, tm, half_k), allr),
        pl.BlockSpec((1, tm, half_k), allr),
        pl.BlockSpec((1, tm, d), lat),
        pl.BlockSpec((1, tm, d), lat),
        pl.BlockSpec((1, tm, d), lat),
        pl.BlockSpec((1, tm, d), lat),
        pl.BlockSpec((1, tm, d), lat),
        pl.BlockSpec((1, tm, d), allr),
        pl.BlockSpec((1, d, tm), lambda b, t: (b, 0, t)),
    ]
    sd = jax.ShapeDtypeStruct
    out_shape = [
        sd((b_, l_, half_k), BF16), sd((b_, lt, half_k), BF16), sd((b_, d, lt), BF16),
        sd((b_, lt, half_k), F32), sd((b_, lt, half_k), F32),
        sd((b_, l_, d), BF16), sd((b_, l_, d), BF16), sd((b_, l_, d), BF16), sd((b_, l_, d), BF16),
        sd((b_, l_, d), BF16), sd((b_, lt, d), BF16), sd((b_, d, lt), BF16),
    ]
    return pl.pallas_call(
        functools.partial(_proj_kernel, offs),
        grid=(b_, nt),
        in_specs=in_specs, out_specs=out_specs, out_shape=out_shape,
        compiler_params=pltpu.CompilerParams(dimension_semantics=("arbitrary", "arbitrary"),
                                             vmem_limit_bytes=VMEM_LIMIT),
        name="proj",
    )(x, ctx, mod3, mod3, w_a, w_b, w_lr, wdec, bdec, qgain, kgain, cos, sin_signed, trif, trib, gsum)


def _chunk_rows(rows, dk):
    return jnp.concatenate([jnp.broadcast_to(r, (GLA_CHUNK, dk)) for r in rows], axis=0)


def _gla_direction(forward, q_ref, k_ref, vt_ref, c_ref, o_ref, st_ref):
    tm = k_ref.shape[1]
    nh = GLA_HEADS
    dk = k_ref.shape[2] // nh
    dv = vt_ref.shape[1] // nh
    nc = tm // GLA_CHUNK
    half = GLA_CHUNK // 2
    ri = lax.broadcasted_iota(jnp.int32, (tm, tm), 0)
    ci = lax.broadcasted_iota(jnp.int32, (tm, tm), 1)
    same = (ri // GLA_CHUNK) == (ci // GLA_CHUNK)
    mask = same & ((ri >= ci) if forward else (ci >= ri))
    row_chunk = lax.broadcasted_iota(jnp.int32, (tm, dk), 0) // GLA_CHUNK
    order = list(range(nc)) if forward else list(reversed(range(nc)))
    zrow = jnp.zeros((1, dk), F32)
    for h in range(nh):
        ks = slice(h * dk, (h + 1) * dk)
        vs = slice(h * dv, (h + 1) * dv)
        cum = c_ref[0, :, ks]
        q = q_ref[0, :, ks].astype(F32)
        k = k_ref[0, :, ks].astype(F32)
        vt = vt_ref[0, vs, :]
        last, ref = [], []
        for c in range(nc):
            base = c * GLA_CHUNK
            li = base + GLA_CHUNK - 1 if forward else base
            ce = base + half - 1 if forward else base + half
            last.append(cum[li:li + 1])
            ref.append(cum[ce:ce + 1])
        lastb = _chunk_rows(last, dk)
        refb = _chunk_rows(ref, dk)
        qe = (q * jnp.exp(cum - refb)).astype(BF16)
        ke = (k * jnp.exp(refb - cum)).astype(BF16)
        a = jnp.where(mask, _dot_nt(qe, ke), 0.0).astype(BF16)
        o_intra = _dot_nt(a, vt)
        p_start, p_end, run = {}, {}, zrow
        for c in order:
            p_start[c] = run
            run = run + last[c]
            p_end[c] = run
        p_total = run
        qd = q * jnp.exp(cum)
        kd = k * jnp.exp(lastb - cum)
        sources = order[:-1]
        k_cols = [jnp.where(row_chunk == s, kd, 0.0).astype(BF16) for s in sources]
        k_cols.append((kd * jnp.exp(_chunk_rows([p_total - p_end[c] for c in range(nc)], dk))).astype(BF16))
        ut = _dot(vt, jnp.concatenate(k_cols, axis=1))
        st = st_ref[h]
        wt = jnp.concatenate([st, ut[:, :(nc - 1) * dk]], axis=1).astype(BF16)
        q_cols = [(qd * jnp.exp(_chunk_rows([p_start[c] for c in range(nc)], dk))).astype(BF16)]
        for i, s in enumerate(sources):
            later = order[i + 1:]
            fac = [jnp.exp(p_start[c] - p_end[s]) if c in later else zrow for c in range(nc)]
            q_cols.append((qd * _chunk_rows(fac, dk)).astype(BF16))
        o_inter = _dot_nt(jnp.concatenate(q_cols, axis=1), wt)
        st_ref[h] = st * jnp.exp(p_total) + ut[:, (nc - 1) * dk:]
        o_ref[0, :, vs] = (o_intra + o_inter).astype(o_ref.dtype)


def _gla_kernel(qf_ref, kf_ref, vtf_ref, cf_ref, qb_ref, kb_ref, vtb_ref, cb_ref,
                of_ref, ob_ref, sf_ref, sb_ref):
    @pl.when(pl.program_id(1) == 0)
    def _():
        sf_ref[...] = jnp.zeros_like(sf_ref)
        sb_ref[...] = jnp.zeros_like(sb_ref)

    _gla_direction(True, qf_ref, kf_ref, vtf_ref, cf_ref, of_ref, sf_ref)
    _gla_direction(False, qb_ref, kb_ref, vtb_ref, cb_ref, ob_ref, sb_ref)


def _gla_call(gq, gk, gvt, cumf, cumb):
    b_, l_, kw = gq.shape
    lt = gk.shape[1]
    d = gvt.shape[1]
    tm = ROW_TILE
    nt = lt // tm
    nlat = l_ // tm
    dk = kw // GLA_HEADS
    dv = d // GLA_HEADS

    bwd_blk = lambda s: jnp.where(s == 0, 0, nt - s)
    f_lat = lambda b, s: (b, jnp.maximum(s - 1, 0), 0)
    f_all = lambda b, s: (b, s, 0)
    b_lat = lambda b, s: (b, jnp.where(s == 0, nlat - 1, nlat - s), 0)
    b_all = lambda b, s: (b, bwd_blk(s), 0)
    return pl.pallas_call(
        _gla_kernel,
        grid=(b_, nt),
        in_specs=[
            pl.BlockSpec((1, tm, kw), f_lat), pl.BlockSpec((1, tm, kw), f_all),
            pl.BlockSpec((1, d, tm), lambda b, s: (b, 0, s)),
            pl.BlockSpec((1, tm, kw), f_all),
            pl.BlockSpec((1, tm, kw), b_lat), pl.BlockSpec((1, tm, kw), b_all),
            pl.BlockSpec((1, d, tm), lambda b, s: (b, 0, bwd_blk(s))),
            pl.BlockSpec((1, tm, kw), b_all),
        ],
        out_specs=[pl.BlockSpec((1, tm, d), f_lat), pl.BlockSpec((1, tm, d), b_lat)],
        out_shape=[jax.ShapeDtypeStruct((b_, l_, d), BF16), jax.ShapeDtypeStruct((b_, l_, d), BF16)],
        scratch_shapes=[pltpu.VMEM((GLA_HEADS, dv, dk), F32), pltpu.VMEM((GLA_HEADS, dv, dk), F32)],
        compiler_params=pltpu.CompilerParams(dimension_semantics=("arbitrary", "arbitrary"),
                                             vmem_limit_bytes=VMEM_LIMIT),
        name="gla",
    )(gq, gk, gvt, cumf, gq, gk, gvt, cumb)


def _attn_kernel(tk, scal_ref, q_ref, k_ref, vt_ref, o_ref, acc1, acc2, m1, l1, m2, l2):
    nsub = acc1.shape[0]
    tq = q_ref.shape[1] // nsub
    nk = k_ref.shape[1] // tk
    lam = scal_ref[0]

    def streams(sub):
        q = q_ref[0, sub * tq:(sub + 1) * tq, :]
        lane = lax.broadcasted_iota(jnp.int32, q.shape, 1)
        zero = jnp.zeros_like(q)
        q1 = jnp.where(lane < DIFF_DK, q, zero)
        q2 = jnp.where(lane >= DIFF_DK, q, zero)
        return ((q1, acc1.at[sub], m1.at[sub], l1.at[sub]), (q2, acc2.at[sub], m2.at[sub], l2.at[sub]))

    def finish(sub):
        inv1 = 1.0 / jnp.sum(l1[sub], axis=0, keepdims=True)
        inv2 = 1.0 / jnp.sum(l2[sub], axis=0, keepdims=True)
        o = acc1[sub] * inv1 - lam * (acc2[sub] * inv2)
        o = o * lax.rsqrt(jnp.mean(o * o, axis=0, keepdims=True) + EPS)
        o_ref[0, sub * tq:(sub + 1) * tq, :] = o.T.astype(o_ref.dtype)

    def bounded_tiles(sub):
        st = streams(sub)

        def scores(j):
            kk = k_ref[0, j * tk:(j + 1) * tk, :]
            return [_dot_nt(kk, qq) for qq, _, _, _ in st]

        s_next = scores(0)
        for j in range(nk):
            s_cur = s_next
            if j + 1 < nk:
                s_next = scores(j + 1)
            vt = vt_ref[0, :, j * tk:(j + 1) * tk]
            for s, (_, acc, _, l) in zip(s_cur, st):
                p = jnp.exp2(s)
                l_part = jnp.sum(p.reshape(tk // 8, 8, tq), axis=0)
                pv = _dot(vt, p.astype(BF16))
                l[...] = l_part if j == 0 else l[...] + l_part
                acc[...] = pv if j == 0 else acc[...] + pv

    def online_tiles(sub):
        st = streams(sub)
        for _, acc, m, l in st:
            acc[...] = jnp.zeros_like(acc)
            l[...] = jnp.zeros_like(l)
            m[...] = jnp.full_like(m, -1e30)

        def body(j, carry):
            off = pl.multiple_of(j * tk, tk)
            kk = k_ref[0, pl.ds(off, tk), :]
            vt = vt_ref[0, :, pl.ds(off, tk)]
            for qq, acc, m, l in st:
                s = _dot_nt(kk, qq)
                m_new = jnp.maximum(m[...], jnp.max(s, axis=0, keepdims=True))
                alpha = jnp.exp2(m[...] - m_new)
                p = jnp.exp2(s - m_new)
                l[0:1] = alpha * l[0:1] + jnp.sum(p, axis=0, keepdims=True)
                acc[...] = alpha * acc[...] + _dot(vt, p.astype(BF16))
                m[...] = m_new
            return carry

        lax.fori_loop(0, nk, body, 0)

    bounded = scal_ref[1] != 0.0

    @pl.when(bounded)
    def _():
        for sub in range(nsub):
            bounded_tiles(sub)
            finish(sub)

    @pl.when(jnp.logical_not(bounded))
    def _():
        for sub in range(nsub):
            online_tiles(sub)
            finish(sub)


def _attn_call(scal, dq, dk, dvt, tq, tk, nsub):
    b_, l_, d = dq.shape
    lt = dk.shape[1]
    nh = d // LANES
    tstep = tq * nsub
    assert l_ % tstep == 0 and lt % tk == 0
    return pl.pallas_call(
        functools.partial(_attn_kernel, tk),
        grid=(b_, nh, l_ // tstep),
        in_specs=[
            pl.BlockSpec(memory_space=pltpu.SMEM),
            pl.BlockSpec((1, tstep, LANES), lambda b, h, i: (b, i, h)),
            pl.BlockSpec((1, lt, LANES), lambda b, h, i: (b, 0, h)),
            pl.BlockSpec((1, LANES, lt), lambda b, h, i: (b, h, 0)),
        ],
        out_specs=pl.BlockSpec((1, tstep, LANES), lambda b, h, i: (b, i, h)),
        out_shape=jax.ShapeDtypeStruct((b_, l_, d), BF16),
        scratch_shapes=[pltpu.VMEM((nsub, LANES, tq), F32), pltpu.VMEM((nsub, LANES, tq), F32),
                        pltpu.VMEM((nsub, 1, tq), F32), pltpu.VMEM((nsub, 8, tq), F32),
                        pltpu.VMEM((nsub, 1, tq), F32), pltpu.VMEM((nsub, 8, tq), F32)],
        compiler_params=pltpu.CompilerParams(
            dimension_semantics=("arbitrary", "arbitrary", "arbitrary"),
            vmem_limit_bytes=VMEM_LIMIT),
        name="diffattn",
    )(scal, dq, dk, dvt)


def _combine_kernel(x_ref, g_ref, of_ref, ob_ref, od_ref, gg_ref, dg_ref, mg_ref, md_ref,
                    ggain_ref, dgain_ref, wg_ref, wd_ref, wo_ref, out_ref):
    d = x_ref.shape[2]
    dv = d // GLA_HEADS
    og = of_ref[0].astype(F32) + ob_ref[0].astype(F32)
    normed = []
    for h in range(GLA_HEADS):
        sl = og[:, h * dv:(h + 1) * dv]
        normed.append(sl * lax.rsqrt(jnp.mean(sl * sl, axis=-1, keepdims=True) + EPS))
    a = jnp.concatenate(normed, axis=1) * ggain_ref[...] * gg_ref[0].astype(F32)
    bb = od_ref[0].astype(F32) * dgain_ref[...] * dg_ref[0].astype(F32)
    y = (mg_ref[0].astype(F32) * _dot(a.astype(BF16), wg_ref[...])
         + md_ref[0].astype(F32) * _dot(bb.astype(BF16), wd_ref[...]))
    out_ref[0] = x_ref[0] + g_ref[0] * _dot(y.astype(BF16), wo_ref[...])


def _combine_call(x, mod3, o_f, o_b, o_d, gg, dg, mg, md, ggain, dgain, wg, wd, wo, tm):
    b_, l_, d = x.shape
    row = lambda b, i: (b, i, 0)
    const2 = lambda b, i: (0, 0)
    once = pl.Buffered(1)
    big = pl.BlockSpec((1, tm, d), row)
    wspec = pl.BlockSpec((d, d), const2, pipeline_mode=once)
    return pl.pallas_call(
        _combine_kernel,
        grid=(b_, l_ // tm),
        in_specs=[big, pl.BlockSpec((1, 1, d), lambda b, i: (b, 0, 2)),
                  big, big, big, big, big, big, big,
                  pl.BlockSpec((1, d), const2, pipeline_mode=once),
                  pl.BlockSpec((1, d), const2, pipeline_mode=once),
                  wspec, wspec, wspec],
        out_specs=big,
        out_shape=jax.ShapeDtypeStruct(x.shape, x.dtype),
        compiler_params=pltpu.CompilerParams(dimension_semantics=("arbitrary", "arbitrary"),
                                             vmem_limit_bytes=VMEM_LIMIT),
        name="combine",
    )(x, mod3, o_f, o_b, o_d, gg, dg, mg, md, ggain, dgain, wg, wd, wo)


def _rope_tables(n_tokens, n_ctx):
    f32 = np.float32
    rows = n_tokens // GRID_W
    inv_freq = f32(ROPE_BASE) ** (-np.arange(0, ROPE_AXIS_DIM, 2, dtype=f32) / f32(ROPE_AXIS_DIM))
    ang_r = (np.arange(rows, dtype=f32)[:, None] * inv_freq).astype(f32)
    ang_c = (np.arange(GRID_W, dtype=f32)[:, None] * inv_freq).astype(f32)
    sign = np.where(np.arange(ROPE_AXIS_DIM) < ROPE_AXIS_DIM // 2, -1.0, 1.0)
    two = lambda t: np.concatenate([t, t], axis=-1).astype(f32)

    def table(fn, sgn):
        tr = jnp.broadcast_to(jnp.asarray(two(fn(ang_r)) * sgn, F32)[:, None, :], (rows, GRID_W, ROPE_AXIS_DIM))
        tc = jnp.broadcast_to(jnp.asarray(two(fn(ang_c)) * sgn, F32)[None, :, :], (rows, GRID_W, ROPE_AXIS_DIM))
        return jnp.concatenate([tr, tc, tr, tc], axis=-1).reshape(n_tokens, LANES)

    cos = jnp.concatenate([jnp.ones((n_ctx, LANES), F32), table(np.cos, 1.0)], axis=0)
    sin = jnp.concatenate([jnp.zeros((n_ctx, LANES), F32), table(np.sin, sign)], axis=0)
    return cos, sin


def _block_tri(n, upper):
    i = np.arange(n)
    same = (i[:, None] // GLA_CHUNK) == (i[None, :] // GLA_CHUNK)
    tri = (i[None, :] >= i[:, None]) if upper else (i[:, None] >= i[None, :])
    return jnp.asarray((same & tri).astype(np.float32)).astype(BF16)


def _attn_tiles(l_, lt):
    tq = 1024 if l_ % 1024 == 0 else l_
    nsub = 2 if l_ % (2 * tq) == 0 else 1
    for tk in (768, 512, 384, 256, 128):
        if lt % tk == 0:
            return tq, tk, nsub
    raise ValueError("key length must be a multiple of 128")


def kernel(x, c, ctx, c_ctx, w_ada, b_ada, w_in, gla_w_decay, gla_b_decay, gla_norm, diff_q_norm,
           diff_k_norm, diff_lambda, diff_norm, w_br_gla, w_br_diff, w_out):
    b_, l_, d = x.shape
    n_ctx = ctx.shape[1]
    depth = w_ada.shape[0]
    assert depth == 1, "single-layer block"
    layer = 0
    lam_init = 0.8 - 0.6 * math.exp(-0.3 * layer)
    half_k = d // 2

    cc = jnp.zeros((8, d), F32).at[:b_].set(c).at[b_].set(c_ctx)
    mod, lam_tile = _mod_call(cc, w_ada[layer], b_ada[layer][None, :], diff_lambda[layer], lam_init)
    mod3 = mod.reshape(8, 1, 3 * d)
    lam = lam_tile[0, :1]

    w = w_in[layer]
    sizes = (half_k, half_k, d, d, 2 * GLA_RANK, d, d, d, d, d, d)
    names = ("gq", "gk", "gv", "gg", "lr", "dq", "dk", "dv", "dg", "mg", "md")
    starts = np.concatenate([[0], np.cumsum(sizes)])
    lr_at = names.index("lr")
    lr_lo, lr_hi = int(starts[lr_at]), int(starts[lr_at + 1])
    w_a = w[:, :lr_lo].astype(BF16)
    w_b = w[:, lr_hi:].astype(BF16)
    w_lr = jnp.concatenate([w[:, lr_lo:lr_hi]] * 3 + [jnp.zeros((d, LANES - 6 * GLA_RANK), w.dtype)],
                           axis=1).astype(BF16)
    offs = {"lr": (2, 0, LANES)}
    for i, name in enumerate(names):
        if i < lr_at:
            offs[name] = (0, int(starts[i]), int(starts[i + 1]))
        elif i > lr_at:
            offs[name] = (1, int(starts[i]) - lr_hi, int(starts[i + 1]) - lr_hi)

    zk = jnp.zeros((GLA_RANK, half_k), F32)
    wd_f32 = jnp.concatenate([jnp.concatenate([gla_w_decay[layer, 0], zk], axis=1),
                              jnp.concatenate([zk, gla_w_decay[layer, 1]], axis=1)], axis=0)
    wd_hi = wd_f32.astype(BF16)
    wd_lo = (wd_f32 - wd_hi.astype(F32)).astype(BF16)
    wdec = jnp.concatenate([wd_hi, wd_lo, wd_hi,
                            jnp.zeros((LANES - 6 * GLA_RANK, 2 * half_k), BF16)], axis=0)
    bdec = jnp.concatenate([gla_b_decay[layer, 0], gla_b_decay[layer, 1]])[None, :]
    reps = d // DIFF_DK
    qgain = (jnp.tile(diff_q_norm[layer], reps) * (DIFF_DK ** -0.5 * LOG2E))[None, :]
    kgain = jnp.tile(diff_k_norm[layer], reps)[None, :]
    cos, sin_signed = _rope_tables(l_, n_ctx)
    trif = _block_tri(ROW_TILE, upper=False)
    trib = _block_tri(ROW_TILE, upper=True)
    gi = np.arange(256)
    gsum = jnp.asarray((gi[:, None] // DIFF_DK == gi[None, :] // DIFF_DK).astype(np.float32)).astype(BF16)

    (gq, gk, gvt, cumf, cumb, gg, dg, mg, md, dq, dk, dvt) = _proj_call(
        x, ctx, mod3, w_a, w_b, w_lr, offs, wdec, bdec, qgain, kgain, cos, sin_signed, trif, trib, gsum)

    o_f, o_b = _gla_call(gq, gk, gvt, cumf, cumb)

    tq, tk, nsub = _attn_tiles(l_, l_ + n_ctx)
    score_bound = DIFF_DK ** 0.5 * jnp.max(jnp.abs(diff_q_norm[layer])) * jnp.max(jnp.abs(diff_k_norm[layer]))
    bounded = (score_bound * BOUND_MARGIN <= SCORE_BOUND).astype(F32)
    scal = jnp.concatenate([lam, bounded[None]])
    o_d = _attn_call(scal, dq, dk, dvt, tq, tk, nsub)

    ggain = jnp.tile(gla_norm[layer], GLA_HEADS)[None, :]
    dgain = (jnp.tile(diff_norm[layer], d // diff_norm.shape[1]) * (1.0 - lam_init))[None, :]
    tm = 512 if l_ % 512 == 0 else l_
    return _combine_call(x, mod3, o_f, o_b, o_d, gg, dg, mg, md, ggain, dgain,
                         w_br_gla[layer].astype(BF16), w_br_diff[layer].astype(BF16),
                         w_out[layer].astype(BF16), tm)
```

```python
import functools
import math

import jax
import jax.numpy as jnp
import numpy as np
from jax import lax
from jax.experimental import pallas as pl
from jax.experimental.pallas import tpu as pltpu

F32 = jnp.float32
BF16 = jnp.bfloat16
HIGHEST = lax.Precision.HIGHEST

EPS = 1e-6
GRID_W = 64
GLA_HEADS = 4
GLA_RANK = 16
GLA_TAU = 16.0
GLA_CHUNK = 64
DIFF_DK = 64
ROPE_BASE = 10000.0
ROPE_AXIS_DIM = DIFF_DK // 2
LANES = 128
SUBLANES = 8
MXU_TILE = 256
ROW_TILE = 256
VMEM_LIMIT = 56 * 1024 * 1024
LOG2E = math.log2(math.e)
SCORE_BOUND = 40.0
BOUND_MARGIN = 1.02


def _dot(a, b):
    return jnp.dot(a, b, preferred_element_type=F32)


def _dot_nt(a, b):
    return lax.dot_general(a, b, (((1,), (1,)), ((), ())), preferred_element_type=F32)


def _dot_exact(a, b):
    return lax.dot_general(a, b, (((1,), (0,)), ((), ())), precision=HIGHEST,
                           preferred_element_type=F32)


def _split_bf16(a):
    hi = a.astype(BF16)
    return hi, (a - hi.astype(F32)).astype(BF16)


def _sigmoid(t):
    return 0.5 * jnp.tanh(0.5 * t) + 0.5


def _silu(t):
    return t * _sigmoid(t)


def _mod_kernel(lam_init, cc_ref, w_ref, b_ref, lamp_ref, mod_ref, lam_ref):
    cc = cc_ref[...]
    mod_ref[...] = _dot_exact(_silu(cc), w_ref[...]) + b_ref[...]
    lp = lamp_ref[...]
    e1 = jnp.exp(jnp.sum(lp[0:1] * lp[1:2], axis=-1, keepdims=True))
    e2 = jnp.exp(jnp.sum(lp[2:3] * lp[3:4], axis=-1, keepdims=True))
    lam_ref[...] = jnp.broadcast_to(e1 - e2 + lam_init, lam_ref.shape)


def _mod_call(cc, w_ada, b_ada, lam_params, lam_init):
    rows, d = cc.shape
    n = w_ada.shape[1]
    return pl.pallas_call(
        functools.partial(_mod_kernel, lam_init),
        grid=(n // d,),
        in_specs=[
            pl.BlockSpec((rows, d), lambda j: (0, 0)),
            pl.BlockSpec((d, d), lambda j: (0, j)),
            pl.BlockSpec((1, d), lambda j: (0, j)),
            pl.BlockSpec(lam_params.shape, lambda j: (0, 0)),
        ],
        out_specs=[
            pl.BlockSpec((rows, d), lambda j: (0, j)),
            pl.BlockSpec((rows, LANES), lambda j: (0, 0)),
        ],
        out_shape=[jax.ShapeDtypeStruct((rows, n), F32), jax.ShapeDtypeStruct((rows, LANES), F32)],
        compiler_params=pltpu.CompilerParams(dimension_semantics=("arbitrary",)),
        name="mod",
    )(cc, w_ada, b_ada, lam_params)


def _group_norm_rope(p, gsum_ref, gain_ref, cos, sin_signed, lo_mask):
    tm, d = p.shape
    sq = (p * p).astype(BF16)
    slabs = []
    for s in range(d // MXU_TILE):
        slabs.append(_dot(sq[:, s * MXU_TILE:(s + 1) * MXU_TILE], gsum_ref[...]))
    ss = jnp.concatenate(slabs, axis=1)
    n = p * lax.rsqrt(ss * (1.0 / DIFF_DK) + EPS) * gain_ref[...]
    outs = []
    for h in range(d // LANES):
        sl = n[:, h * LANES:(h + 1) * LANES]
        up = pltpu.roll(sl, LANES - ROPE_AXIS_DIM // 2, axis=1)
        dn = pltpu.roll(sl, ROPE_AXIS_DIM // 2, axis=1)
        outs.append(sl * cos + jnp.where(lo_mask, up, dn) * sin_signed)
    return jnp.concatenate(outs, axis=1)


def _proj_kernel(offs, x_ref, ctx_ref, shift_ref, scale_ref, wa_ref, wb_ref, wlr_ref, wdec_ref, bdec_ref,
                 qgain_ref, kgain_ref, cos_ref, sin_ref, trif_ref, trib_ref, gsum_ref,
                 gq_ref, gk_ref, gvt_ref, cf_ref, cb_ref, gg_ref, dg_ref, mg_ref, md_ref,
                 dq_ref, dk_ref, dvt_ref):
    t = pl.program_id(1)
    latent = t > 0
    xin = jnp.where(latent, x_ref[0], ctx_ref[0])
    ms = jnp.mean(xin * xin, axis=-1, keepdims=True)
    h = xin * lax.rsqrt(ms + EPS) * (1.0 + scale_ref[0]) + shift_ref[0]
    hb = h.astype(BF16)

    def proj(name):
        which, a, b = offs[name]
        return _dot(hb, (wa_ref, wb_ref, wlr_ref)[which][:, a:b])

    lr3 = proj("lr")
    lr_hi, lr_lo = _split_bf16(lr3)
    lane = lax.broadcasted_iota(jnp.int32, lr3.shape, 1)
    z = _dot(jnp.where(lane < 4 * GLA_RANK, lr_hi, lr_lo), wdec_ref[...]) + bdec_ref[...]
    la = (jnp.minimum(z, 0.0) - jnp.log(1.0 + jnp.exp(-jnp.abs(z)))) * (1.0 / GLA_TAU)
    half = la.shape[1] // 2
    la_hi, la_lo = _split_bf16(la)

    dk_gla = gq_ref.shape[2] // GLA_HEADS
    gq_ref[0] = (proj("gq") * dk_gla ** -0.5).astype(BF16)
    gk_ref[0] = proj("gk").astype(BF16)
    gvt_ref[0] = proj("gv").T.astype(BF16)

    for name, ref in (("gg", gg_ref), ("dg", dg_ref)):
        hg = proj(name)
        ref[0] = (hg * jnp.tanh(hg) + hg).astype(BF16)
    for name, ref in (("mg", mg_ref), ("md", md_ref)):
        ref[0] = (0.5 * jnp.tanh(proj(name)) + 0.5).astype(BF16)

    cf_ref[0] = _dot(trif_ref[...], la_hi[:, :half]) + _dot(trif_ref[...], la_lo[:, :half])
    cb_ref[0] = _dot(trib_ref[...], la_hi[:, half:]) + _dot(trib_ref[...], la_lo[:, half:])

    cos = cos_ref[...]
    sin_signed = sin_ref[...]
    lane = lax.broadcasted_iota(jnp.int32, cos.shape, 1)
    lo_mask = (lane % ROPE_AXIS_DIM) < (ROPE_AXIS_DIM // 2)
    dq_ref[0] = _group_norm_rope(proj("dq"), gsum_ref, qgain_ref, cos, sin_signed, lo_mask).astype(BF16)
    dk_ref[0] = _group_norm_rope(proj("dk"), gsum_ref, kgain_ref, cos, sin_signed, lo_mask).astype(BF16)
    dvt_ref[0] = proj("dv").T.astype(BF16)


def _proj_call(x, ctx, mod3, w_a, w_b, w_lr, offs, wdec, bdec, qgain, kgain, cos, sin_signed,
               trif, trib, gsum):
    b_, l_, d = x.shape
    tm = ROW_TILE
    assert ctx.shape[1] == tm and l_ % tm == 0
    nt = l_ // tm + 1
    lt = l_ + tm
    n_ctx_row = b_
    half_k = offs["gq"][2] - offs["gq"][1]

    lat = lambda b, t: (b, jnp.maximum(t - 1, 0), 0)
    allr = lambda b, t: (b, t, 0)
    const2 = lambda b, t: (0, 0)
    once = pl.Buffered(1)

    def mod_spec(col):
        return pl.BlockSpec((1, 1, d), lambda b, t: (jnp.where(t == 0, n_ctx_row, b), 0, col))

    in_specs = [
        pl.BlockSpec((1, tm, d), lat),
        pl.BlockSpec((1, tm, d), lambda b, t: (b, 0, 0)),
        mod_spec(0), mod_spec(1),
        pl.BlockSpec(w_a.shape, const2, pipeline_mode=once),
        pl.BlockSpec(w_b.shape, const2, pipeline_mode=once),
        pl.BlockSpec(w_lr.shape, const2, pipeline_mode=once),
        pl.BlockSpec(wdec.shape, const2, pipeline_mode=once),
        pl.BlockSpec(bdec.shape, const2, pipeline_mode=once),
        pl.BlockSpec(qgain.shape, const2, pipeline_mode=once),
        pl.BlockSpec(kgain.shape, const2, pipeline_mode=once),
        pl.BlockSpec((tm, LANES), lambda b, t: (t, 0)),
        pl.BlockSpec((tm, LANES), lambda b, t: (t, 0)),
        pl.BlockSpec(trif.shape, const2, pipeline_mode=once),
        pl.BlockSpec(trib.shape, const2, pipeline_mode=once),
        pl.BlockSpec(gsum.shape, const2, pipeline_mode=once),
    ]
    out_specs = [
        pl.BlockSpec((1, tm, half_k), lat),
        pl.BlockSpec((1, tm, half_k), allr),
        pl.BlockSpec((1, d, tm), lambda b, t: (b, 0, t)),
        pl.BlockSpec((1, tm, half_k), allr),
        pl.BlockSpec((1, tm, half_k), allr),
        pl.BlockSpec((1, tm, d), lat),
        pl.BlockSpec((1, tm, d), lat),
        pl.BlockSpec((1, tm, d), lat),
        pl.BlockSpec((1, tm, d), lat),
        pl.BlockSpec((1, tm, d), lat),
        pl.BlockSpec((1, tm, d), allr),
        pl.BlockSpec((1, d, tm), lambda b, t: (b, 0, t)),
    ]
    sd = jax.ShapeDtypeStruct
    out_shape = [
        sd((b_, l_, half_k), BF16), sd((b_, lt, half_k), BF16), sd((b_, d, lt), BF16),
        sd((b_, lt, half_k), F32), sd((b_, lt, half_k), F32),
        sd((b_, l_, d), BF16), sd((b_, l_, d), BF16), sd((b_, l_, d), BF16), sd((b_, l_, d), BF16),
        sd((b_, l_, d), BF16), sd((b_, lt, d), BF16), sd((b_, d, lt), BF16),
    ]
    return pl.pallas_call(
        functools.partial(_proj_kernel, offs),
        grid=(b_, nt),
        in_specs=in_specs, out_specs=out_specs, out_shape=out_shape,
        compiler_params=pltpu.CompilerParams(dimension_semantics=("arbitrary", "arbitrary"),
                                             vmem_limit_bytes=VMEM_LIMIT),
        name="proj",
    )(x, ctx, mod3, mod3, w_a, w_b, w_lr, wdec, bdec, qgain, kgain, cos, sin_signed, trif, trib, gsum)


def _chunk_rows(rows, dk):
    return jnp.concatenate([jnp.broadcast_to(r, (GLA_CHUNK, dk)) for r in rows], axis=0)


def _gla_direction(forward, q_ref, k_ref, vt_ref, c_ref, o_ref, st_ref):
    tm = k_ref.shape[1]
    nh = GLA_HEADS
    dk = k_ref.shape[2] // nh
    dv = vt_ref.shape[1] // nh
    nc = tm // GLA_CHUNK
    half = GLA_CHUNK // 2
    ri = lax.broadcasted_iota(jnp.int32, (tm, tm), 0)
    ci = lax.broadcasted_iota(jnp.int32, (tm, tm), 1)
    same = (ri // GLA_CHUNK) == (ci // GLA_CHUNK)
    mask = same & ((ri >= ci) if forward else (ci >= ri))
    row_chunk = lax.broadcasted_iota(jnp.int32, (tm, dk), 0) // GLA_CHUNK
    order = list(range(nc)) if forward else list(reversed(range(nc)))
    zrow = jnp.zeros((1, dk), F32)
    for h in range(nh):
        ks = slice(h * dk, (h + 1) * dk)
        vs = slice(h * dv, (h + 1) * dv)
        cum = c_ref[0, :, ks]
        q = q_ref[0, :, ks].astype(F32)
        k = k_ref[0, :, ks].astype(F32)
        vt = vt_ref[0, vs, :]
        last, ref = [], []
        for c in range(nc):
            base = c * GLA_CHUNK
            li = base + GLA_CHUNK - 1 if forward else base
            ce = base + half - 1 if forward else base + half
            last.append(cum[li:li + 1])
            ref.append(cum[ce:ce + 1])
        lastb = _chunk_rows(last, dk)
        refb = _chunk_rows(ref, dk)
        qe = (q * jnp.exp(cum - refb)).astype(BF16)
        ke = (k * jnp.exp(refb - cum)).astype(BF16)
        a = jnp.where(mask, _dot_nt(qe, ke), 0.0).astype(BF16)
        o_intra = _dot_nt(a, vt)
        p_start, p_end, run = {}, {}, zrow
        for c in order:
            p_start[c] = run
            run = run + last[c]
            p_end[c] = run
        p_total = run
        qd = q * jnp.exp(cum)
        kd = k * jnp.exp(lastb - cum)
        sources = order[:-1]
        k_cols = [jnp.where(row_chunk == s, kd, 0.0).astype(BF16) for s in sources]
        k_cols.append((kd * jnp.exp(_chunk_rows([p_total - p_end[c] for c in range(nc)], dk))).astype(BF16))
        ut = _dot(vt, jnp.concatenate(k_cols, axis=1))
        st = st_ref[h]
        wt = jnp.concatenate([st, ut[:, :(nc - 1) * dk]], axis=1).astype(BF16)
        q_cols = [(qd * jnp.exp(_chunk_rows([p_start[c] for c in range(nc)], dk))).astype(BF16)]
        for i, s in enumerate(sources):
            later = order[i + 1:]
            fac = [jnp.exp(p_start[c] - p_end[s]) if c in later else zrow for c in range(nc)]
            q_cols.append((qd * _chunk_rows(fac, dk)).astype(BF16))
        o_inter = _dot_nt(jnp.concatenate(q_cols, axis=1), wt)
        st_ref[h] = st * jnp.exp(p_total) + ut[:, (nc - 1) * dk:]
        o_ref[0, :, vs] = (o_intra + o_inter).astype(o_ref.dtype)


def _gla_kernel(qf_ref, kf_ref, vtf_ref, cf_ref, qb_ref, kb_ref, vtb_ref, cb_ref,
                of_ref, ob_ref, sf_ref, sb_ref):
    @pl.when(pl.program_id(1) == 0)
    def _():
        sf_ref[...] = jnp.zeros_like(sf_ref)
        sb_ref[...] = jnp.zeros_like(sb_ref)

    _gla_direction(True, qf_ref, kf_ref, vtf_ref, cf_ref, of_ref, sf_ref)
    _gla_direction(False, qb_ref, kb_ref, vtb_ref, cb_ref, ob_ref, sb_ref)


def _gla_call(gq, gk, gvt, cumf, cumb):
    b_, l_, kw = gq.shape
    lt = gk.shape[1]
    d = gvt.shape[1]
    tm = ROW_TILE
    nt = lt // tm
    nlat = l_ // tm
    dk = kw // GLA_HEADS
    dv = d // GLA_HEADS

    bwd_blk = lambda s: jnp.where(s == 0, 0, nt - s)
    f_lat = lambda b, s: (b, jnp.maximum(s - 1, 0), 0)
    f_all = lambda b, s: (b, s, 0)
    b_lat = lambda b, s: (b, jnp.where(s == 0, nlat - 1, nlat - s), 0)
    b_all = lambda b, s: (b, bwd_blk(s), 0)
    return pl.pallas_call(
        _gla_kernel,
        grid=(b_, nt),
        in_specs=[
            pl.BlockSpec((1, tm, kw), f_lat), pl.BlockSpec((1, tm, kw), f_all),
            pl.BlockSpec((1, d, tm), lambda b, s: (b, 0, s)),
            pl.BlockSpec((1, tm, kw), f_all),
            pl.BlockSpec((1, tm, kw), b_lat), pl.BlockSpec((1, tm, kw), b_all),
            pl.BlockSpec((1, d, tm), lambda b, s: (b, 0, bwd_blk(s))),
            pl.BlockSpec((1, tm, kw), b_all),
        ],
        out_specs=[pl.BlockSpec((1, tm, d), f_lat), pl.BlockSpec((1, tm, d), b_lat)],
        out_shape=[jax.ShapeDtypeStruct((b_, l_, d), BF16), jax.ShapeDtypeStruct((b_, l_, d), BF16)],
        scratch_shapes=[pltpu.VMEM((GLA_HEADS, dv, dk), F32), pltpu.VMEM((GLA_HEADS, dv, dk), F32)],
        compiler_params=pltpu.CompilerParams(dimension_semantics=("arbitrary", "arbitrary"),
                                             vmem_limit_bytes=VMEM_LIMIT),
        name="gla",
    )(gq, gk, gvt, cumf, gq, gk, gvt, cumb)


def _attn_kernel(tk, scal_ref, q_ref, k_ref, vt_ref, o_ref, acc1, acc2, m1, l1, m2, l2):
    nsub = acc1.shape[0]
    tq = q_ref.shape[1] // nsub
    nk = k_ref.shape[1] // tk
    lam = scal_ref[0]

    def streams(sub):
        q = q_ref[0, sub * tq:(sub + 1) * tq, :]
        lane = lax.broadcasted_iota(jnp.int32, q.shape, 1)
        zero = jnp.zeros_like(q)
        q1 = jnp.where(lane < DIFF_DK, q, zero)
        q2 = jnp.where(lane >= DIFF_DK, q, zero)
        return ((q1, acc1.at[sub], m1.at[sub], l1.at[sub]), (q2, acc2.at[sub], m2.at[sub], l2.at[sub]))

    def finish(sub):
        inv1 = 1.0 / jnp.sum(l1[sub], axis=0, keepdims=True)
        inv2 = 1.0 / jnp.sum(l2[sub], axis=0, keepdims=True)
        o = acc1[sub] * inv1 - lam * (acc2[sub] * inv2)
        o = o * lax.rsqrt(jnp.mean(o * o, axis=0, keepdims=True) + EPS)
        o_ref[0, sub * tq:(sub + 1) * tq, :] = o.T.astype(o_ref.dtype)

    def bounded_tiles(sub):
        st = streams(sub)

        def scores(j):
            kk = k_ref[0, j * tk:(j + 1) * tk, :]
            return [_dot_nt(kk, qq) for qq, _, _, _ in st]

        s_next = scores(0)
        for j in range(nk):
            s_cur = s_next
            if j + 1 < nk:
                s_next = scores(j + 1)
            vt = vt_ref[0, :, j * tk:(j + 1) * tk]
            for s, (_, acc, _, l) in zip(s_cur, st):
                p = jnp.exp2(s)
                l_part = jnp.sum(p.reshape(tk // SUBLANES, SUBLANES, tq), axis=0)
                pv = _dot(vt, p.astype(BF16))
                l[...] = l_part if j == 0 else l[...] + l_part
                acc[...] = pv if j == 0 else acc[...] + pv

    def online_tiles(sub):
        st = streams(sub)
        for _, acc, m, l in st:
            acc[...] = jnp.zeros_like(acc)
            l[...] = jnp.zeros_like(l)
            m[...] = jnp.full_like(m, -1e30)

        def body(j, carry):
            off = pl.multiple_of(j * tk, tk)
            kk = k_ref[0, pl.ds(off, tk), :]
            vt = vt_ref[0, :, pl.ds(off, tk)]
            for qq, acc, m, l in st:
                s = _dot_nt(kk, qq)
                m_new = jnp.maximum(m[...], jnp.max(s, axis=0, keepdims=True))
                alpha = jnp.exp2(m[...] - m_new)
                p = jnp.exp2(s - m_new)
                l[0:1] = alpha * l[0:1] + jnp.sum(p, axis=0, keepdims=True)
                acc[...] = alpha * acc[...] + _dot(vt, p.astype(BF16))
                m[...] = m_new
            return carry

        lax.fori_loop(0, nk, body, 0)

    bounded = scal_ref[1] != 0.0

    @pl.when(bounded)
    def _():
        for sub in range(nsub):
            bounded_tiles(sub)
            finish(sub)

    @pl.when(jnp.logical_not(bounded))
    def _():
        for sub in range(nsub):
            online_tiles(sub)
            finish(sub)


def _attn_call(scal, dq, dk, dvt, tq, tk, nsub):
    b_, l_, d = dq.shape
    lt = dk.shape[1]
    nh = d // LANES
    tstep = tq * nsub
    assert l_ % tstep == 0 and lt % tk == 0
    return pl.pallas_call(
        functools.partial(_attn_kernel, tk),
        grid=(b_, nh, l_ // tstep),
        in_specs=[
            pl.BlockSpec(memory_space=pltpu.SMEM),
            pl.BlockSpec((1, tstep, LANES), lambda b, h, i: (b, i, h)),
            pl.BlockSpec((1, lt, LANES), lambda b, h, i: (b, 0, h)),
            pl.BlockSpec((1, LANES, lt), lambda b, h, i: (b, h, 0)),
        ],
        out_specs=pl.BlockSpec((1, tstep, LANES), lambda b, h, i: (b, i, h)),
        out_shape=jax.ShapeDtypeStruct((b_, l_, d), BF16),
        scratch_shapes=[pltpu.VMEM((nsub, LANES, tq), F32), pltpu.VMEM((nsub, LANES, tq), F32),
                        pltpu.VMEM((nsub, 1, tq), F32), pltpu.VMEM((nsub, SUBLANES, tq), F32),
                        pltpu.VMEM((nsub, 1, tq), F32), pltpu.VMEM((nsub, SUBLANES, tq), F32)],
        compiler_params=pltpu.CompilerParams(
            dimension_semantics=("arbitrary", "arbitrary", "arbitrary"),
            vmem_limit_bytes=VMEM_LIMIT),
        name="diffattn",
    )(scal, dq, dk, dvt)


def _combine_kernel(x_ref, g_ref, of_ref, ob_ref, od_ref, gg_ref, dg_ref, mg_ref, md_ref,
                    ggain_ref, dgain_ref, wg_ref, wd_ref, wo_ref, out_ref):
    d = x_ref.shape[2]
    dv = d // GLA_HEADS
    og = of_ref[0].astype(F32) + ob_ref[0].astype(F32)
    normed = []
    for h in range(GLA_HEADS):
        sl = og[:, h * dv:(h + 1) * dv]
        normed.append(sl * lax.rsqrt(jnp.mean(sl * sl, axis=-1, keepdims=True) + EPS))
    a = jnp.concatenate(normed, axis=1) * ggain_ref[...] * gg_ref[0].astype(F32)
    bb = od_ref[0].astype(F32) * dgain_ref[...] * dg_ref[0].astype(F32)
    y = (mg_ref[0].astype(F32) * _dot(a.astype(BF16), wg_ref[...])
         + md_ref[0].astype(F32) * _dot(bb.astype(BF16), wd_ref[...]))
    out_ref[0] = x_ref[0] + g_ref[0] * _dot(y.astype(BF16), wo_ref[...])


def _combine_call(x, mod3, o_f, o_b, o_d, gg, dg, mg, md, ggain, dgain, wg, wd, wo, tm):
    b_, l_, d = x.shape
    row = lambda b, i: (b, i, 0)
    const2 = lambda b, i: (0, 0)
    once = pl.Buffered(1)
    big = pl.BlockSpec((1, tm, d), row)
    wspec = pl.BlockSpec((d, d), const2, pipeline_mode=once)
    return pl.pallas_call(
        _combine_kernel,
        grid=(b_, l_ // tm),
        in_specs=[big, pl.BlockSpec((1, 1, d), lambda b, i: (b, 0, 2)),
                  big, big, big, big, big, big, big,
                  pl.BlockSpec((1, d), const2, pipeline_mode=once),
                  pl.BlockSpec((1, d), const2, pipeline_mode=once),
                  wspec, wspec, wspec],
        out_specs=big,
        out_shape=jax.ShapeDtypeStruct(x.shape, x.dtype),
        compiler_params=pltpu.CompilerParams(dimension_semantics=("arbitrary", "arbitrary"),
                                             vmem_limit_bytes=VMEM_LIMIT),
        name="combine",
    )(x, mod3, o_f, o_b, o_d, gg, dg, mg, md, ggain, dgain, wg, wd, wo)


def _rope_tables(n_tokens, n_ctx):
    f32 = np.float32
    rows = n_tokens // GRID_W
    inv_freq = f32(ROPE_BASE) ** (-np.arange(0, ROPE_AXIS_DIM, 2, dtype=f32) / f32(ROPE_AXIS_DIM))
    ang_r = (np.arange(rows, dtype=f32)[:, None] * inv_freq).astype(f32)
    ang_c = (np.arange(GRID_W, dtype=f32)[:, None] * inv_freq).astype(f32)
    sign = np.where(np.arange(ROPE_AXIS_DIM) < ROPE_AXIS_DIM // 2, -1.0, 1.0)
    two = lambda t: np.concatenate([t, t], axis=-1).astype(f32)

    def table(fn, sgn):
        tr = jnp.broadcast_to(jnp.asarray(two(fn(ang_r)) * sgn, F32)[:, None, :], (rows, GRID_W, ROPE_AXIS_DIM))
        tc = jnp.broadcast_to(jnp.asarray(two(fn(ang_c)) * sgn, F32)[None, :, :], (rows, GRID_W, ROPE_AXIS_DIM))
        return jnp.concatenate([tr, tc, tr, tc], axis=-1).reshape(n_tokens, LANES)

    cos = jnp.concatenate([jnp.ones((n_ctx, LANES), F32), table(np.cos, 1.0)], axis=0)
    sin = jnp.concatenate([jnp.zeros((n_ctx, LANES), F32), table(np.sin, sign)], axis=0)
    return cos, sin


def _block_tri(n, upper):
    i = np.arange(n)
    same = (i[:, None] // GLA_CHUNK) == (i[None, :] // GLA_CHUNK)
    tri = (i[None, :] >= i[:, None]) if upper else (i[:, None] >= i[None, :])
    return jnp.asarray((same & tri).astype(np.float32)).astype(BF16)


def _attn_tiles(l_, lt):
    tq = 1024 if l_ % 1024 == 0 else l_
    nsub = 2 if l_ % (2 * tq) == 0 else 1
    for tk in (768, 512, 384, 256, 128):
        if lt % tk == 0:
            return tq, tk, nsub
    raise ValueError("key length must be a multiple of 128")


def kernel(x, c, ctx, c_ctx, w_ada, b_ada, w_in, gla_w_decay, gla_b_decay, gla_norm, diff_q_norm,
           diff_k_norm, diff_lambda, diff_norm, w_br_gla, w_br_diff, w_out):
    b_, l_, d = x.shape
    n_ctx = ctx.shape[1]
    depth = w_ada.shape[0]
    assert depth == 1, "single-layer block"
    layer = 0
    lam_init = 0.8 - 0.6 * math.exp(-0.3 * layer)
    half_k = d // 2

    n_mod = -(-(b_ + 1) // SUBLANES) * SUBLANES
    cc = jnp.zeros((n_mod, d), F32).at[:b_].set(c).at[b_].set(c_ctx)
    mod, lam_tile = _mod_call(cc, w_ada[layer], b_ada[layer][None, :], diff_lambda[layer], lam_init)
    mod3 = mod.reshape(n_mod, 1, 3 * d)
    lam = lam_tile[0, :1]

    w = w_in[layer]
    sizes = (half_k, half_k, d, d, 2 * GLA_RANK, d, d, d, d, d, d)
    names = ("gq", "gk", "gv", "gg", "lr", "dq", "dk", "dv", "dg", "mg", "md")
    starts = np.concatenate([[0], np.cumsum(sizes)])
    lr_at = names.index("lr")
    lr_lo, lr_hi = int(starts[lr_at]), int(starts[lr_at + 1])
    col_scale = np.ones((1, int(starts[-1])), np.float32)
    for name in ("gg", "dg", "mg", "md"):
        i = names.index(name)
        col_scale[:, int(starts[i]):int(starts[i + 1])] = 0.5
    w_a = (w[:, :lr_lo] * col_scale[:, :lr_lo]).astype(BF16)
    w_b = (w[:, lr_hi:] * col_scale[:, lr_hi:]).astype(BF16)
    w_lr = jnp.concatenate([w[:, lr_lo:lr_hi]] * 3 + [jnp.zeros((d, LANES - 6 * GLA_RANK), w.dtype)],
                           axis=1).astype(BF16)
    offs = {"lr": (2, 0, LANES)}
    for i, name in enumerate(names):
        if i < lr_at:
            offs[name] = (0, int(starts[i]), int(starts[i + 1]))
        elif i > lr_at:
            offs[name] = (1, int(starts[i]) - lr_hi, int(starts[i + 1]) - lr_hi)

    zk = jnp.zeros((GLA_RANK, half_k), F32)
    wd_f32 = jnp.concatenate([jnp.concatenate([gla_w_decay[layer, 0], zk], axis=1),
                              jnp.concatenate([zk, gla_w_decay[layer, 1]], axis=1)], axis=0)
    wd_hi = wd_f32.astype(BF16)
    wd_lo = (wd_f32 - wd_hi.astype(F32)).astype(BF16)
    wdec = jnp.concatenate([wd_hi, wd_lo, wd_hi,
                            jnp.zeros((LANES - 6 * GLA_RANK, 2 * half_k), BF16)], axis=0)
    bdec = jnp.concatenate([gla_b_decay[layer, 0], gla_b_decay[layer, 1]])[None, :]
    reps = d // DIFF_DK
    qgain = (jnp.tile(diff_q_norm[layer], reps) * (DIFF_DK ** -0.5 * LOG2E))[None, :]
    kgain = jnp.tile(diff_k_norm[layer], reps)[None, :]
    cos, sin_signed = _rope_tables(l_, n_ctx)
    trif = _block_tri(ROW_TILE, upper=False)
    trib = _block_tri(ROW_TILE, upper=True)
    gi = np.arange(MXU_TILE)
    gsum = jnp.asarray((gi[:, None] // DIFF_DK == gi[None, :] // DIFF_DK).astype(np.float32)).astype(BF16)

    (gq, gk, gvt, cumf, cumb, gg, dg, mg, md, dq, dk, dvt) = _proj_call(
        x, ctx, mod3, w_a, w_b, w_lr, offs, wdec, bdec, qgain, kgain, cos, sin_signed, trif, trib, gsum)

    o_f, o_b = _gla_call(gq, gk, gvt, cumf, cumb)

    tq, tk, nsub = _attn_tiles(l_, l_ + n_ctx)
    score_bound = DIFF_DK ** 0.5 * jnp.max(jnp.abs(diff_q_norm[layer])) * jnp.max(jnp.abs(diff_k_norm[layer]))
    bounded = (score_bound * BOUND_MARGIN <= SCORE_BOUND).astype(F32)
    scal = jnp.concatenate([lam, bounded[None]])
    o_d = _attn_call(scal, dq, dk, dvt, tq, tk, nsub)

    ggain = jnp.tile(gla_norm[layer], GLA_HEADS)[None, :]
    dgain = (jnp.tile(diff_norm[layer], d // diff_norm.shape[1]) * (1.0 - lam_init))[None, :]
    tm = 512 if l_ % 512 == 0 else l_
    return _combine_call(x, mod3, o_f, o_b, o_d, gg, dg, mg, md, ggain, dgain,
                         w_br_gla[layer].astype(BF16), w_br_diff[layer].astype(BF16),
                         w_out[layer].astype(BF16), tm)
```

```python
import functools
import math

import jax
import jax.numpy as jnp
import numpy as np
from jax import lax
from jax.experimental import pallas as pl
from jax.experimental.pallas import tpu as pltpu

F32 = jnp.float32
BF16 = jnp.bfloat16
HIGHEST = lax.Precision.HIGHEST

EPS = 1e-6
GRID_W = 64
GLA_HEADS = 4
GLA_RANK = 16
GLA_TAU = 16.0
GLA_CHUNK = 64
DIFF_DK = 64
ROPE_BASE = 10000.0
ROPE_AXIS_DIM = DIFF_DK // 2
LANES = 128
SUBLANES = 8
MXU_TILE = 256
ROW_TILE = 256
VMEM_LIMIT = 56 * 1024 * 1024
LOG2E = math.log2(math.e)
SCORE_BOUND = 40.0
BOUND_MARGIN = 1.02


def _dot(a, b):
    return jnp.dot(a, b, preferred_element_type=F32)


def _dot_nt(a, b):
    return lax.dot_general(a, b, (((1,), (1,)), ((), ())), preferred_element_type=F32)


def _dot_exact(a, b):
    return lax.dot_general(a, b, (((1,), (0,)), ((), ())), precision=HIGHEST,
                           preferred_element_type=F32)


def _split_bf16(a):
    hi = a.astype(BF16)
    return hi, (a - hi.astype(F32)).astype(BF16)


def _sigmoid(t):
    return 0.5 * jnp.tanh(0.5 * t) + 0.5


def _silu(t):
    return t * _sigmoid(t)


def _mod_kernel(lam_init, cc_ref, w_ref, b_ref, lamp_ref, mod_ref, lam_ref):
    cc = cc_ref[...]
    mod_ref[...] = _dot_exact(_silu(cc), w_ref[...]) + b_ref[...]
    lp = lamp_ref[...]
    e1 = jnp.exp(jnp.sum(lp[0:1] * lp[1:2], axis=-1, keepdims=True))
    e2 = jnp.exp(jnp.sum(lp[2:3] * lp[3:4], axis=-1, keepdims=True))
    lam_ref[...] = jnp.broadcast_to(e1 - e2 + lam_init, lam_ref.shape)


def _mod_call(cc, w_ada, b_ada, lam_params, lam_init):
    rows, d = cc.shape
    n = w_ada.shape[1]
    return pl.pallas_call(
        functools.partial(_mod_kernel, lam_init),
        grid=(n // d,),
        in_specs=[
            pl.BlockSpec((rows, d), lambda j: (0, 0)),
            pl.BlockSpec((d, d), lambda j: (0, j)),
            pl.BlockSpec((1, d), lambda j: (0, j)),
            pl.BlockSpec(lam_params.shape, lambda j: (0, 0)),
        ],
        out_specs=[
            pl.BlockSpec((rows, d), lambda j: (0, j)),
            pl.BlockSpec((rows, LANES), lambda j: (0, 0)),
        ],
        out_shape=[jax.ShapeDtypeStruct((rows, n), F32), jax.ShapeDtypeStruct((rows, LANES), F32)],
        compiler_params=pltpu.CompilerParams(dimension_semantics=("arbitrary",)),
        name="mod",
    )(cc, w_ada, b_ada, lam_params)


def _group_norm_rope(p, gsum_ref, cos_g, sin_g, lo_mask):
    tm, d = p.shape
    sq = (p * p).astype(BF16)
    slabs = []
    for s in range(d // MXU_TILE):
        slabs.append(_dot(sq[:, s * MXU_TILE:(s + 1) * MXU_TILE], gsum_ref[...]))
    n = p * lax.rsqrt(jnp.concatenate(slabs, axis=1) + EPS)
    outs = []
    for h in range(d // LANES):
        sl = n[:, h * LANES:(h + 1) * LANES]
        up = pltpu.roll(sl, LANES - ROPE_AXIS_DIM // 2, axis=1)
        dn = pltpu.roll(sl, ROPE_AXIS_DIM // 2, axis=1)
        outs.append(sl * cos_g + jnp.where(lo_mask, up, dn) * sin_g)
    return jnp.concatenate(outs, axis=1)


def _proj_kernel(offs, x_ref, ctx_ref, shift_ref, scale_ref, wa_ref, wb_ref, wlr_ref, wdec_ref,
                 cosq_ref, sinq_ref, cosk_ref, sink_ref, trif_ref, trib_ref, gsum_ref,
                 gq_ref, gk_ref, gvt_ref, cf_ref, cb_ref, gg_ref, dg_ref, mg_ref, md_ref,
                 dq_ref, dk_ref, dvt_ref):
    t = pl.program_id(1)
    latent = t > 0
    xin = jnp.where(latent, x_ref[0], ctx_ref[0])
    ms = jnp.mean(xin * xin, axis=-1, keepdims=True)
    h = xin * lax.rsqrt(ms + EPS) * (1.0 + scale_ref[0]) + shift_ref[0]
    hb = h.astype(BF16)

    def proj(name):
        which, a, b = offs[name]
        return _dot(hb, (wa_ref, wb_ref, wlr_ref)[which][:, a:b])

    lr3 = proj("lr")
    lr_hi, lr_lo = _split_bf16(lr3)
    lane = lax.broadcasted_iota(jnp.int32, lr3.shape, 1)
    lhs = jnp.where(lane < 4 * GLA_RANK, lr_hi,
                    jnp.where(lane < 6 * GLA_RANK, lr_lo, jnp.ones_like(lr_lo)))
    z = _dot(lhs, wdec_ref[...])
    la = jnp.minimum(z, 0.0) - jnp.log(1.0 + jnp.exp(-jnp.abs(z)))
    half = la.shape[1] // 2
    la_hi, la_lo = _split_bf16(la)

    gq_ref[0] = proj("gq").astype(BF16)
    gk_ref[0] = proj("gk").astype(BF16)
    gvt_ref[0] = proj("gv").T.astype(BF16)

    for name, ref in (("gg", gg_ref), ("dg", dg_ref)):
        hg = proj(name)
        ref[0] = (hg * jnp.tanh(hg) + hg).astype(BF16)
    for name, ref in (("mg", mg_ref), ("md", md_ref)):
        ref[0] = (0.5 * jnp.tanh(proj(name)) + 0.5).astype(BF16)

    cf_ref[0] = _dot(trif_ref[...], la_hi[:, :half]) + _dot(trif_ref[...], la_lo[:, :half])
    cb_ref[0] = _dot(trib_ref[...], la_hi[:, half:]) + _dot(trib_ref[...], la_lo[:, half:])

    lane = lax.broadcasted_iota(jnp.int32, cosq_ref.shape, 1)
    lo_mask = (lane % ROPE_AXIS_DIM) < (ROPE_AXIS_DIM // 2)
    dq_ref[0] = _group_norm_rope(proj("dq"), gsum_ref, cosq_ref[...], sinq_ref[...], lo_mask).astype(BF16)
    dk_ref[0] = _group_norm_rope(proj("dk"), gsum_ref, cosk_ref[...], sink_ref[...], lo_mask).astype(BF16)
    dvt_ref[0] = proj("dv").T.astype(BF16)


def _proj_call(x, ctx, mod3, w_a, w_b, w_lr, offs, wdec, cos_q, sin_q, cos_k, sin_k,
               trif, trib, gsum):
    b_, l_, d = x.shape
    tm = ROW_TILE
    assert ctx.shape[1] == tm and l_ % tm == 0
    nt = l_ // tm + 1
    lt = l_ + tm
    n_ctx_row = b_
    half_k = offs["gq"][2] - offs["gq"][1]

    lat = lambda b, t: (b, jnp.maximum(t - 1, 0), 0)
    allr = lambda b, t: (b, t, 0)
    const2 = lambda b, t: (0, 0)
    once = pl.Buffered(1)

    def mod_spec(col):
        return pl.BlockSpec((1, 1, d), lambda b, t: (jnp.where(t == 0, n_ctx_row, b), 0, col))

    in_specs = [
        pl.BlockSpec((1, tm, d), lat),
        pl.BlockSpec((1, tm, d), lambda b, t: (b, 0, 0)),
        mod_spec(0), mod_spec(1),
        pl.BlockSpec(w_a.shape, const2, pipeline_mode=once),
        pl.BlockSpec(w_b.shape, const2, pipeline_mode=once),
        pl.BlockSpec(w_lr.shape, const2, pipeline_mode=once),
        pl.BlockSpec(wdec.shape, const2, pipeline_mode=once),
        pl.BlockSpec((tm, LANES), lambda b, t: (t, 0)),
        pl.BlockSpec((tm, LANES), lambda b, t: (t, 0)),
        pl.BlockSpec((tm, LANES), lambda b, t: (t, 0)),
        pl.BlockSpec((tm, LANES), lambda b, t: (t, 0)),
        pl.BlockSpec(trif.shape, const2, pipeline_mode=once),
        pl.BlockSpec(trib.shape, const2, pipeline_mode=once),
        pl.BlockSpec(gsum.shape, const2, pipeline_mode=once),
    ]
    out_specs = [
        pl.BlockSpec((1, tm, half_k), lat),
        pl.BlockSpec((1, tm, half_k), allr),
        pl.BlockSpec((1, d, tm), lambda b, t: (b, 0, t)),
        pl.BlockSpec((1, tm, half_k), allr),
        pl.BlockSpec((1, tm, half_k), allr),
        pl.BlockSpec((1, tm, d), lat),
        pl.BlockSpec((1, tm, d), lat),
        pl.BlockSpec((1, tm, d), lat),
        pl.BlockSpec((1, tm, d), lat),
        pl.BlockSpec((1, tm, d), lat),
        pl.BlockSpec((1, tm, d), allr),
        pl.BlockSpec((1, d, tm), lambda b, t: (b, 0, t)),
    ]
    sd = jax.ShapeDtypeStruct
    out_shape = [
        sd((b_, l_, half_k), BF16), sd((b_, lt, half_k), BF16), sd((b_, d, lt), BF16),
        sd((b_, lt, half_k), F32), sd((b_, lt, half_k), F32),
        sd((b_, l_, d), BF16), sd((b_, l_, d), BF16), sd((b_, l_, d), BF16), sd((b_, l_, d), BF16),
        sd((b_, l_, d), BF16), sd((b_, lt, d), BF16), sd((b_, d, lt), BF16),
    ]
    return pl.pallas_call(
        functools.partial(_proj_kernel, offs),
        grid=(b_, nt),
        in_specs=in_specs, out_specs=out_specs, out_shape=out_shape,
        compiler_params=pltpu.CompilerParams(dimension_semantics=("arbitrary", "arbitrary"),
                                             vmem_limit_bytes=VMEM_LIMIT),
        name="proj",
    )(x, ctx, mod3, mod3, w_a, w_b, w_lr, wdec, cos_q, sin_q, cos_k, sin_k, trif, trib, gsum)


def _chunk_rows(rows, dk):
    return jnp.concatenate([jnp.broadcast_to(r, (GLA_CHUNK, dk)) for r in rows], axis=0)


def _gla_direction(forward, q_ref, k_ref, vt_ref, c_ref, o_ref, st_ref):
    tm = k_ref.shape[1]
    nh = GLA_HEADS
    dk = k_ref.shape[2] // nh
    dv = vt_ref.shape[1] // nh
    nc = tm // GLA_CHUNK
    half = GLA_CHUNK // 2
    ri = lax.broadcasted_iota(jnp.int32, (tm, tm), 0)
    ci = lax.broadcasted_iota(jnp.int32, (tm, tm), 1)
    same = (ri // GLA_CHUNK) == (ci // GLA_CHUNK)
    mask = same & ((ri >= ci) if forward else (ci >= ri))
    row_chunk = lax.broadcasted_iota(jnp.int32, (tm, dk), 0) // GLA_CHUNK
    order = list(range(nc)) if forward else list(reversed(range(nc)))
    zrow = jnp.zeros((1, dk), F32)
    for h in range(nh):
        ks = slice(h * dk, (h + 1) * dk)
        vs = slice(h * dv, (h + 1) * dv)
        cum = c_ref[0, :, ks]
        q = q_ref[0, :, ks].astype(F32)
        k = k_ref[0, :, ks].astype(F32)
        vt = vt_ref[0, vs, :]
        last, ref = [], []
        for c in range(nc):
            base = c * GLA_CHUNK
            li = base + GLA_CHUNK - 1 if forward else base
            ce = base + half - 1 if forward else base + half
            last.append(cum[li:li + 1])
            ref.append(cum[ce:ce + 1])
        lastb = _chunk_rows(last, dk)
        refb = _chunk_rows(ref, dk)
        qe = (q * jnp.exp(cum - refb)).astype(BF16)
        ke = (k * jnp.exp(refb - cum)).astype(BF16)
        a = jnp.where(mask, _dot_nt(qe, ke), 0.0).astype(BF16)
        o_intra = _dot_nt(a, vt)
        p_start, p_end, run = {}, {}, zrow
        for c in order:
            p_start[c] = run
            run = run + last[c]
            p_end[c] = run
        p_total = run
        qd = q * jnp.exp(cum)
        kd = k * jnp.exp(lastb - cum)
        sources = order[:-1]
        k_cols = [jnp.where(row_chunk == s, kd, 0.0).astype(BF16) for s in sources]
        k_cols.append((kd * jnp.exp(_chunk_rows([p_total - p_end[c] for c in range(nc)], dk))).astype(BF16))
        ut = _dot(vt, jnp.concatenate(k_cols, axis=1))
        st = st_ref[h]
        wt = jnp.concatenate([st, ut[:, :(nc - 1) * dk]], axis=1).astype(BF16)
        q_cols = [(qd * jnp.exp(_chunk_rows([p_start[c] for c in range(nc)], dk))).astype(BF16)]
        for i, s in enumerate(sources):
            later = order[i + 1:]
            fac = [jnp.exp(p_start[c] - p_end[s]) if c in later else zrow for c in range(nc)]
            q_cols.append((qd * _chunk_rows(fac, dk)).astype(BF16))
        o_inter = _dot_nt(jnp.concatenate(q_cols, axis=1), wt)
        st_ref[h] = st * jnp.exp(p_total) + ut[:, (nc - 1) * dk:]
        o_ref[0, :, vs] = (o_intra + o_inter).astype(o_ref.dtype)


def _gla_kernel(qf_ref, kf_ref, vtf_ref, cf_ref, qb_ref, kb_ref, vtb_ref, cb_ref,
                of_ref, ob_ref, sf_ref, sb_ref):
    @pl.when(pl.program_id(1) == 0)
    def _():
        sf_ref[...] = jnp.zeros_like(sf_ref)
        sb_ref[...] = jnp.zeros_like(sb_ref)

    _gla_direction(True, qf_ref, kf_ref, vtf_ref, cf_ref, of_ref, sf_ref)
    _gla_direction(False, qb_ref, kb_ref, vtb_ref, cb_ref, ob_ref, sb_ref)


def _gla_call(gq, gk, gvt, cumf, cumb):
    b_, l_, kw = gq.shape
    lt = gk.shape[1]
    d = gvt.shape[1]
    tm = ROW_TILE
    nt = lt // tm
    nlat = l_ // tm
    dk = kw // GLA_HEADS
    dv = d // GLA_HEADS

    bwd_blk = lambda s: jnp.where(s == 0, 0, nt - s)
    f_lat = lambda b, s: (b, jnp.maximum(s - 1, 0), 0)
    f_all = lambda b, s: (b, s, 0)
    b_lat = lambda b, s: (b, jnp.where(s == 0, nlat - 1, nlat - s), 0)
    b_all = lambda b, s: (b, bwd_blk(s), 0)
    return pl.pallas_call(
        _gla_kernel,
        grid=(b_, nt),
        in_specs=[
            pl.BlockSpec((1, tm, kw), f_lat), pl.BlockSpec((1, tm, kw), f_all),
            pl.BlockSpec((1, d, tm), lambda b, s: (b, 0, s)),
            pl.BlockSpec((1, tm, kw), f_all),
            pl.BlockSpec((1, tm, kw), b_lat), pl.BlockSpec((1, tm, kw), b_all),
            pl.BlockSpec((1, d, tm), lambda b, s: (b, 0, bwd_blk(s))),
            pl.BlockSpec((1, tm, kw), b_all),
        ],
        out_specs=[pl.BlockSpec((1, tm, d), f_lat), pl.BlockSpec((1, tm, d), b_lat)],
        out_shape=[jax.ShapeDtypeStruct((b_, l_, d), BF16), jax.ShapeDtypeStruct((b_, l_, d), BF16)],
        scratch_shapes=[pltpu.VMEM((GLA_HEADS, dv, dk), F32), pltpu.VMEM((GLA_HEADS, dv, dk), F32)],
        compiler_params=pltpu.CompilerParams(dimension_semantics=("arbitrary", "arbitrary"),
                                             vmem_limit_bytes=VMEM_LIMIT),
        name="gla",
    )(gq, gk, gvt, cumf, gq, gk, gvt, cumb)


def _attn_kernel(tk, scal_ref, q_ref, k_ref, vt_ref, o_ref, acc1, acc2, m1, l1, m2, l2):
    nsub = acc1.shape[0]
    tq = q_ref.shape[1] // nsub
    nk = k_ref.shape[1] // tk
    lam = scal_ref[0]

    def streams(sub):
        q = q_ref[0, sub * tq:(sub + 1) * tq, :]
        lane = lax.broadcasted_iota(jnp.int32, q.shape, 1)
        zero = jnp.zeros_like(q)
        q1 = jnp.where(lane < DIFF_DK, q, zero)
        q2 = jnp.where(lane >= DIFF_DK, q, zero)
        return ((q1, acc1.at[sub], m1.at[sub], l1.at[sub]), (q2, acc2.at[sub], m2.at[sub], l2.at[sub]))

    def finish(sub):
        inv1 = 1.0 / jnp.sum(l1[sub], axis=0, keepdims=True)
        inv2 = 1.0 / jnp.sum(l2[sub], axis=0, keepdims=True)
        o = acc1[sub] * inv1 - lam * (acc2[sub] * inv2)
        o = o * lax.rsqrt(jnp.mean(o * o, axis=0, keepdims=True) + EPS)
        o_ref[0, sub * tq:(sub + 1) * tq, :] = o.T.astype(o_ref.dtype)

    def bounded_tiles(sub):
        st = streams(sub)

        def scores(j):
            kk = k_ref[0, j * tk:(j + 1) * tk, :]
            return [_dot_nt(kk, qq) for qq, _, _, _ in st]

        s_next = scores(0)
        for j in range(nk):
            s_cur = s_next
            if j + 1 < nk:
                s_next = scores(j + 1)
            vt = vt_ref[0, :, j * tk:(j + 1) * tk]
            for s, (_, acc, _, l) in zip(s_cur, st):
                p = jnp.exp2(s)
                l_part = jnp.sum(p.reshape(tk // SUBLANES, SUBLANES, tq), axis=0)
                pv = _dot(vt, p.astype(BF16))
                l[...] = l_part if j == 0 else l[...] + l_part
                acc[...] = pv if j == 0 else acc[...] + pv

    def online_tiles(sub):
        st = streams(sub)
        for _, acc, m, l in st:
            acc[...] = jnp.zeros_like(acc)
            l[...] = jnp.zeros_like(l)
            m[...] = jnp.full_like(m, -1e30)

        def body(j, carry):
            off = pl.multiple_of(j * tk, tk)
            kk = k_ref[0, pl.ds(off, tk), :]
            vt = vt_ref[0, :, pl.ds(off, tk)]
            for qq, acc, m, l in st:
                s = _dot_nt(kk, qq)
                m_new = jnp.maximum(m[...], jnp.max(s, axis=0, keepdims=True))
                alpha = jnp.exp2(m[...] - m_new)
                p = jnp.exp2(s - m_new)
                l[0:1] = alpha * l[0:1] + jnp.sum(p, axis=0, keepdims=True)
                acc[...] = alpha * acc[...] + _dot(vt, p.astype(BF16))
                m[...] = m_new
            return carry

        lax.fori_loop(0, nk, body, 0)

    bounded = scal_ref[1] != 0.0

    @pl.when(bounded)
    def _():
        for sub in range(nsub):
            bounded_tiles(sub)
            finish(sub)

    @pl.when(jnp.logical_not(bounded))
    def _():
        for sub in range(nsub):
            online_tiles(sub)
            finish(sub)


def _attn_call(scal, dq, dk, dvt, tq, tk, nsub):
    b_, l_, d = dq.shape
    lt = dk.shape[1]
    nh = d // LANES
    tstep = tq * nsub
    assert l_ % tstep == 0 and lt % tk == 0
    return pl.pallas_call(
        functools.partial(_attn_kernel, tk),
        grid=(b_, nh, l_ // tstep),
        in_specs=[
            pl.BlockSpec(memory_space=pltpu.SMEM),
            pl.BlockSpec((1, tstep, LANES), lambda b, h, i: (b, i, h)),
            pl.BlockSpec((1, lt, LANES), lambda b, h, i: (b, 0, h)),
            pl.BlockSpec((1, LANES, lt), lambda b, h, i: (b, h, 0)),
        ],
        out_specs=pl.BlockSpec((1, tstep, LANES), lambda b, h, i: (b, i, h)),
        out_shape=jax.ShapeDtypeStruct((b_, l_, d), BF16),
        scratch_shapes=[pltpu.VMEM((nsub, LANES, tq), F32), pltpu.VMEM((nsub, LANES, tq), F32),
                        pltpu.VMEM((nsub, 1, tq), F32), pltpu.VMEM((nsub, SUBLANES, tq), F32),
                        pltpu.VMEM((nsub, 1, tq), F32), pltpu.VMEM((nsub, SUBLANES, tq), F32)],
        compiler_params=pltpu.CompilerParams(
            dimension_semantics=("arbitrary", "arbitrary", "arbitrary"),
            vmem_limit_bytes=VMEM_LIMIT),
        name="diffattn",
    )(scal, dq, dk, dvt)


def _combine_kernel(x_ref, g_ref, of_ref, ob_ref, od_ref, gg_ref, dg_ref, mg_ref, md_ref,
                    ggain_ref, dgain_ref, wg_ref, wd_ref, wo_ref, out_ref):
    d = x_ref.shape[2]
    dv = d // GLA_HEADS
    og = of_ref[0].astype(F32) + ob_ref[0].astype(F32)
    normed = []
    for h in range(GLA_HEADS):
        sl = og[:, h * dv:(h + 1) * dv]
        normed.append(sl * lax.rsqrt(jnp.mean(sl * sl, axis=-1, keepdims=True) + EPS))
    a = jnp.concatenate(normed, axis=1) * ggain_ref[...] * gg_ref[0].astype(F32)
    bb = od_ref[0].astype(F32) * dgain_ref[...] * dg_ref[0].astype(F32)
    y = (mg_ref[0].astype(F32) * _dot(a.astype(BF16), wg_ref[...])
         + md_ref[0].astype(F32) * _dot(bb.astype(BF16), wd_ref[...]))
    out_ref[0] = x_ref[0] + g_ref[0] * _dot(y.astype(BF16), wo_ref[...])


def _combine_call(x, mod3, o_f, o_b, o_d, gg, dg, mg, md, ggain, dgain, wg, wd, wo, tm):
    b_, l_, d = x.shape
    row = lambda b, i: (b, i, 0)
    const2 = lambda b, i: (0, 0)
    once = pl.Buffered(1)
    big = pl.BlockSpec((1, tm, d), row)
    wspec = pl.BlockSpec((d, d), const2, pipeline_mode=once)
    return pl.pallas_call(
        _combine_kernel,
        grid=(b_, l_ // tm),
        in_specs=[big, pl.BlockSpec((1, 1, d), lambda b, i: (b, 0, 2)),
                  big, big, big, big, big, big, big,
                  pl.BlockSpec((1, d), const2, pipeline_mode=once),
                  pl.BlockSpec((1, d), const2, pipeline_mode=once),
                  wspec, wspec, wspec],
        out_specs=big,
        out_shape=jax.ShapeDtypeStruct(x.shape, x.dtype),
        compiler_params=pltpu.CompilerParams(dimension_semantics=("arbitrary", "arbitrary"),
                                             vmem_limit_bytes=VMEM_LIMIT),
        name="combine",
    )(x, mod3, o_f, o_b, o_d, gg, dg, mg, md, ggain, dgain, wg, wd, wo)


def _rope_tables(n_tokens, n_ctx):
    f32 = np.float32
    rows = n_tokens // GRID_W
    inv_freq = f32(ROPE_BASE) ** (-np.arange(0, ROPE_AXIS_DIM, 2, dtype=f32) / f32(ROPE_AXIS_DIM))
    ang_r = (np.arange(rows, dtype=f32)[:, None] * inv_freq).astype(f32)
    ang_c = (np.arange(GRID_W, dtype=f32)[:, None] * inv_freq).astype(f32)
    sign = np.where(np.arange(ROPE_AXIS_DIM) < ROPE_AXIS_DIM // 2, -1.0, 1.0)
    two = lambda t: np.concatenate([t, t], axis=-1).astype(f32)

    def table(fn, sgn):
        tr = jnp.broadcast_to(jnp.asarray(two(fn(ang_r)) * sgn, F32)[:, None, :], (rows, GRID_W, ROPE_AXIS_DIM))
        tc = jnp.broadcast_to(jnp.asarray(two(fn(ang_c)) * sgn, F32)[None, :, :], (rows, GRID_W, ROPE_AXIS_DIM))
        return jnp.concatenate([tr, tc, tr, tc], axis=-1).reshape(n_tokens, LANES)

    cos = jnp.concatenate([jnp.ones((n_ctx, LANES), F32), table(np.cos, 1.0)], axis=0)
    sin = jnp.concatenate([jnp.zeros((n_ctx, LANES), F32), table(np.sin, sign)], axis=0)
    return cos, sin


def _block_tri(n, upper):
    i = np.arange(n)
    same = (i[:, None] // GLA_CHUNK) == (i[None, :] // GLA_CHUNK)
    tri = (i[None, :] >= i[:, None]) if upper else (i[:, None] >= i[None, :])
    return jnp.asarray((same & tri).astype(np.float32)).astype(BF16)


def _attn_tiles(l_, lt):
    tq = 1024 if l_ % 1024 == 0 else l_
    nsub = next(n for n in (4, 2, 1) if l_ % (n * tq) == 0)
    for tk in (768, 512, 384, 256, 128):
        if lt % tk == 0:
            return tq, tk, nsub
    raise ValueError("key length must be a multiple of 128")


def kernel(x, c, ctx, c_ctx, w_ada, b_ada, w_in, gla_w_decay, gla_b_decay, gla_norm, diff_q_norm,
           diff_k_norm, diff_lambda, diff_norm, w_br_gla, w_br_diff, w_out):
    b_, l_, d = x.shape
    n_ctx = ctx.shape[1]
    depth = w_ada.shape[0]
    assert depth == 1, "single-layer block"
    layer = 0
    lam_init = 0.8 - 0.6 * math.exp(-0.3 * layer)
    half_k = d // 2

    n_mod = -(-(b_ + 1) // SUBLANES) * SUBLANES
    cc = jnp.zeros((n_mod, d), F32).at[:b_].set(c).at[b_].set(c_ctx)
    mod, lam_tile = _mod_call(cc, w_ada[layer], b_ada[layer][None, :], diff_lambda[layer], lam_init)
    mod3 = mod.reshape(n_mod, 1, 3 * d)
    lam = lam_tile[0, :1]

    w = w_in[layer]
    sizes = (half_k, half_k, d, d, 2 * GLA_RANK, d, d, d, d, d, d)
    names = ("gq", "gk", "gv", "gg", "lr", "dq", "dk", "dv", "dg", "mg", "md")
    starts = np.concatenate([[0], np.cumsum(sizes)])
    lr_at = names.index("lr")
    lr_lo, lr_hi = int(starts[lr_at]), int(starts[lr_at + 1])
    col_scale = np.ones((1, int(starts[-1])), np.float32)
    for name in ("gg", "dg", "mg", "md"):
        i = names.index(name)
        col_scale[:, int(starts[i]):int(starts[i + 1])] = 0.5
    col_scale[:, int(starts[0]):int(starts[1])] = (half_k // GLA_HEADS) ** -0.5
    w_a = (w[:, :lr_lo] * col_scale[:, :lr_lo]).astype(BF16)
    w_b = (w[:, lr_hi:] * col_scale[:, lr_hi:]).astype(BF16)
    w_lr = jnp.concatenate([w[:, lr_lo:lr_hi]] * 3 + [jnp.zeros((d, LANES - 6 * GLA_RANK), w.dtype)],
                           axis=1).astype(BF16)
    offs = {"lr": (2, 0, LANES)}
    for i, name in enumerate(names):
        if i < lr_at:
            offs[name] = (0, int(starts[i]), int(starts[i + 1]))
        elif i > lr_at:
            offs[name] = (1, int(starts[i]) - lr_hi, int(starts[i + 1]) - lr_hi)

    zk = jnp.zeros((GLA_RANK, half_k), F32)
    wd_f32 = jnp.concatenate([jnp.concatenate([gla_w_decay[layer, 0], zk], axis=1),
                              jnp.concatenate([zk, gla_w_decay[layer, 1]], axis=1)], axis=0)
    bd_f32 = jnp.concatenate([gla_b_decay[layer, 0], gla_b_decay[layer, 1]])[None, :]
    hi_lo = lambda t: (t.astype(BF16), (t - t.astype(BF16).astype(F32)).astype(BF16))
    (wd_hi, wd_lo), (bd_hi, bd_lo) = hi_lo(wd_f32), hi_lo(bd_f32)
    wdec = jnp.concatenate([wd_hi, wd_lo, wd_hi, bd_hi, bd_lo,
                            jnp.zeros((LANES - 6 * GLA_RANK - 2, 2 * half_k), BF16)], axis=0)

    cos, sin_signed = _rope_tables(l_, n_ctx)
    lane = np.arange(LANES)
    partner = np.where(lane % ROPE_AXIS_DIM < ROPE_AXIS_DIM // 2,
                       lane + ROPE_AXIS_DIM // 2, lane - ROPE_AXIS_DIM // 2)
    g_q = jnp.tile(diff_q_norm[layer], LANES // DIFF_DK) * (DIFF_DK ** -0.5 * LOG2E)
    g_k = jnp.tile(diff_k_norm[layer], LANES // DIFF_DK)
    cos_q, sin_q = cos * g_q[None, :], sin_signed * g_q[partner][None, :]
    cos_k, sin_k = cos * g_k[None, :], sin_signed * g_k[partner][None, :]

    trif = _block_tri(ROW_TILE, upper=False) * (1.0 / GLA_TAU)
    trib = _block_tri(ROW_TILE, upper=True) * (1.0 / GLA_TAU)
    gi = np.arange(MXU_TILE)
    gsum = jnp.asarray((gi[:, None] // DIFF_DK == gi[None, :] // DIFF_DK).astype(np.float32)
                       / DIFF_DK).astype(BF16)

    (gq, gk, gvt, cumf, cumb, gg, dg, mg, md, dq, dk, dvt) = _proj_call(
        x, ctx, mod3, w_a, w_b, w_lr, offs, wdec, cos_q, sin_q, cos_k, sin_k, trif, trib, gsum)

    o_f, o_b = _gla_call(gq, gk, gvt, cumf, cumb)

    tq, tk, nsub = _attn_tiles(l_, l_ + n_ctx)
    score_bound = DIFF_DK ** 0.5 * jnp.max(jnp.abs(diff_q_norm[layer])) * jnp.max(jnp.abs(diff_k_norm[layer]))
    bounded = (score_bound * BOUND_MARGIN <= SCORE_BOUND).astype(F32)
    scal = jnp.concatenate([lam, bounded[None]])
    o_d = _attn_call(scal, dq, dk, dvt, tq, tk, nsub)

    ggain = jnp.tile(gla_norm[layer], GLA_HEADS)[None, :]
    dgain = (jnp.tile(diff_norm[layer], d // diff_norm.shape[1]) * (1.0 - lam_init))[None, :]
    tm = 512 if l_ % 512 == 0 else l_
    return _combine_call(x, mod3, o_f, o_b, o_d, gg, dg, mg, md, ggain, dgain,
                         w_br_gla[layer].astype(BF16), w_br_diff[layer].astype(BF16),
                         w_out[layer].astype(BF16), tm)
```

```python
import functools
import math

import jax
import jax.numpy as jnp
import numpy as np
from jax import lax
from jax.experimental import pallas as pl
from jax.experimental.pallas import tpu as pltpu

F32 = jnp.float32
BF16 = jnp.bfloat16
HIGHEST = lax.Precision.HIGHEST

EPS = 1e-6
GRID_W = 64
GLA_HEADS = 4
GLA_RANK = 16
GLA_TAU = 16.0
GLA_CHUNK = 64
DIFF_DK = 64
ROPE_BASE = 10000.0
ROPE_AXIS_DIM = DIFF_DK // 2
LANES = 128
SUBLANES = 8
MXU_TILE = 256
ROW_TILE = 256
VMEM_LIMIT = 56 * 1024 * 1024
LOG2E = math.log2(math.e)
SCORE_BOUND = 40.0
BOUND_MARGIN = 1.02


def _dot(a, b):
    return jnp.dot(a, b, preferred_element_type=F32)


def _dot_nt(a, b):
    return lax.dot_general(a, b, (((1,), (1,)), ((), ())), preferred_element_type=F32)


def _dot_exact(a, b):
    return lax.dot_general(a, b, (((1,), (0,)), ((), ())), precision=HIGHEST,
                           preferred_element_type=F32)


def _split_bf16(a):
    hi = a.astype(BF16)
    return hi, (a - hi.astype(F32)).astype(BF16)


def _sigmoid(t):
    return 0.5 * jnp.tanh(0.5 * t) + 0.5


def _silu(t):
    return t * _sigmoid(t)


def _mod_kernel(lam_init, cc_ref, w_ref, b_ref, lamp_ref, mod_ref, lam_ref):
    cc = cc_ref[...]
    mod_ref[...] = _dot_exact(_silu(cc), w_ref[...]) + b_ref[...]
    lp = lamp_ref[...]
    e1 = jnp.exp(jnp.sum(lp[0:1] * lp[1:2], axis=-1, keepdims=True))
    e2 = jnp.exp(jnp.sum(lp[2:3] * lp[3:4], axis=-1, keepdims=True))
    lam_ref[...] = jnp.broadcast_to(e1 - e2 + lam_init, lam_ref.shape)


def _mod_call(cc, w_ada, b_ada, lam_params, lam_init):
    rows, d = cc.shape
    n = w_ada.shape[1]
    return pl.pallas_call(
        functools.partial(_mod_kernel, lam_init),
        grid=(n // d,),
        in_specs=[
            pl.BlockSpec((rows, d), lambda j: (0, 0)),
            pl.BlockSpec((d, d), lambda j: (0, j)),
            pl.BlockSpec((1, d), lambda j: (0, j)),
            pl.BlockSpec(lam_params.shape, lambda j: (0, 0)),
        ],
        out_specs=[
            pl.BlockSpec((rows, d), lambda j: (0, j)),
            pl.BlockSpec((rows, LANES), lambda j: (0, 0)),
        ],
        out_shape=[jax.ShapeDtypeStruct((rows, n), F32), jax.ShapeDtypeStruct((rows, LANES), F32)],
        compiler_params=pltpu.CompilerParams(dimension_semantics=("arbitrary",)),
        name="mod",
    )(cc, w_ada, b_ada, lam_params)


def _group_norm_rope(p, gsum_ref, cos_g, sin_g, lo_mask):
    tm, d = p.shape
    sq = (p * p).astype(BF16)
    slabs = []
    for s in range(d // MXU_TILE):
        slabs.append(_dot(sq[:, s * MXU_TILE:(s + 1) * MXU_TILE], gsum_ref[...]))
    n = p * lax.rsqrt(jnp.concatenate(slabs, axis=1) + EPS)
    outs = []
    for h in range(d // LANES):
        sl = n[:, h * LANES:(h + 1) * LANES]
        up = pltpu.roll(sl, LANES - ROPE_AXIS_DIM // 2, axis=1)
        dn = pltpu.roll(sl, ROPE_AXIS_DIM // 2, axis=1)
        outs.append(sl * cos_g + jnp.where(lo_mask, up, dn) * sin_g)
    return jnp.concatenate(outs, axis=1)


def _proj_kernel(offs, x_ref, ctx_ref, shift_ref, scale_ref, wa_ref, wb_ref, wlr_ref, wdec_ref,
                 gains_ref, cos_ref, sin_ref, trif_ref, trib_ref, gsum_ref,
                 gq_ref, gk_ref, gvt_ref, cf_ref, cb_ref, gg_ref, dg_ref, mg_ref, md_ref,
                 dq_ref, dk_ref, dvt_ref):
    t = pl.program_id(1)
    latent = t > 0
    xin = jnp.where(latent, x_ref[0], ctx_ref[0])
    ms = jnp.mean(xin * xin, axis=-1, keepdims=True)
    h = xin * lax.rsqrt(ms + EPS) * (1.0 + scale_ref[0]) + shift_ref[0]
    hb = h.astype(BF16)

    def proj(name):
        which, a, b = offs[name]
        return _dot(hb, (wa_ref, wb_ref, wlr_ref)[which][:, a:b])

    lr3 = proj("lr")
    lr_hi, lr_lo = _split_bf16(lr3)
    lane = lax.broadcasted_iota(jnp.int32, lr3.shape, 1)
    lhs = jnp.where(lane < 4 * GLA_RANK, lr_hi,
                    jnp.where(lane < 6 * GLA_RANK, lr_lo, jnp.ones_like(lr_lo)))
    z = _dot(lhs, wdec_ref[...])
    la = jnp.minimum(z, 0.0) - jnp.log(1.0 + jnp.exp(-jnp.abs(z)))
    half = la.shape[1] // 2
    la_hi, la_lo = _split_bf16(la)

    gq_ref[0] = proj("gq").astype(BF16)
    gk_ref[0] = proj("gk").astype(BF16)
    gvt_ref[0] = proj("gv").T.astype(BF16)

    for name, ref in (("gg", gg_ref), ("dg", dg_ref)):
        hg = proj(name)
        ref[0] = (hg * jnp.tanh(hg) + hg).astype(BF16)
    for name, ref in (("mg", mg_ref), ("md", md_ref)):
        ref[0] = (0.5 * jnp.tanh(proj(name)) + 0.5).astype(BF16)

    cf_ref[0] = _dot(trif_ref[...], la_hi[:, :half]) + _dot(trif_ref[...], la_lo[:, :half])
    cb_ref[0] = _dot(trib_ref[...], la_hi[:, half:]) + _dot(trib_ref[...], la_lo[:, half:])

    cos = cos_ref[...]
    sin_signed = sin_ref[...]
    lane = lax.broadcasted_iota(jnp.int32, cos.shape, 1)
    lo_mask = (lane % ROPE_AXIS_DIM) < (ROPE_AXIS_DIM // 2)
    dq_ref[0] = _group_norm_rope(proj("dq"), gsum_ref, cos * gains_ref[0:1], sin_signed * gains_ref[1:2],
                                 lo_mask).astype(BF16)
    dk_ref[0] = _group_norm_rope(proj("dk"), gsum_ref, cos * gains_ref[2:3], sin_signed * gains_ref[3:4],
                                 lo_mask).astype(BF16)
    dvt_ref[0] = proj("dv").T.astype(BF16)


def _proj_call(x, ctx, mod3, w_a, w_b, w_lr, offs, wdec, gains, cos, sin_signed,
               trif, trib, gsum):
    b_, l_, d = x.shape
    tm = ROW_TILE
    assert ctx.shape[1] == tm and l_ % tm == 0
    nt = l_ // tm + 1
    lt = l_ + tm
    n_ctx_row = b_
    half_k = offs["gq"][2] - offs["gq"][1]

    lat = lambda b, t: (b, jnp.maximum(t - 1, 0), 0)
    allr = lambda b, t: (b, t, 0)
    const2 = lambda b, t: (0, 0)
    once = pl.Buffered(1)

    def mod_spec(col):
        return pl.BlockSpec((1, 1, d), lambda b, t: (jnp.where(t == 0, n_ctx_row, b), 0, col))

    in_specs = [
        pl.BlockSpec((1, tm, d), lat),
        pl.BlockSpec((1, tm, d), lambda b, t: (b, 0, 0)),
        mod_spec(0), mod_spec(1),
        pl.BlockSpec(w_a.shape, const2, pipeline_mode=once),
        pl.BlockSpec(w_b.shape, const2, pipeline_mode=once),
        pl.BlockSpec(w_lr.shape, const2, pipeline_mode=once),
        pl.BlockSpec(wdec.shape, const2, pipeline_mode=once),
        pl.BlockSpec(gains.shape, const2, pipeline_mode=once),
        pl.BlockSpec((tm, LANES), lambda b, t: (t, 0)),
        pl.BlockSpec((tm, LANES), lambda b, t: (t, 0)),
        pl.BlockSpec(trif.shape, const2, pipeline_mode=once),
        pl.BlockSpec(trib.shape, const2, pipeline_mode=once),
        pl.BlockSpec(gsum.shape, const2, pipeline_mode=once),
    ]
    out_specs = [
        pl.BlockSpec((1, tm, half_k), lat),
        pl.BlockSpec((1, tm, half_k), allr),
        pl.BlockSpec((1, d, tm), lambda b, t: (b, 0, t)),
        pl.BlockSpec((1, tm, half_k), allr),
        pl.BlockSpec((1, tm, half_k), allr),
        pl.BlockSpec((1, tm, d), lat),
        pl.BlockSpec((1, tm, d), lat),
        pl.BlockSpec((1, tm, d), lat),
        pl.BlockSpec((1, tm, d), lat),
        pl.BlockSpec((1, tm, d), lat),
        pl.BlockSpec((1, tm, d), allr),
        pl.BlockSpec((1, d, tm), lambda b, t: (b, 0, t)),
    ]
    sd = jax.ShapeDtypeStruct
    out_shape = [
        sd((b_, l_, half_k), BF16), sd((b_, lt, half_k), BF16), sd((b_, d, lt), BF16),
        sd((b_, lt, half_k), F32), sd((b_, lt, half_k), F32),
        sd((b_, l_, d), BF16), sd((b_, l_, d), BF16), sd((b_, l_, d), BF16), sd((b_, l_, d), BF16),
        sd((b_, l_, d), BF16), sd((b_, lt, d), BF16), sd((b_, d, lt), BF16),
    ]
    return pl.pallas_call(
        functools.partial(_proj_kernel, offs),
        grid=(b_, nt),
        in_specs=in_specs, out_specs=out_specs, out_shape=out_shape,
        compiler_params=pltpu.CompilerParams(dimension_semantics=("arbitrary", "arbitrary"),
                                             vmem_limit_bytes=VMEM_LIMIT),
        name="proj",
    )(x, ctx, mod3, mod3, w_a, w_b, w_lr, wdec, gains, cos, sin_signed, trif, trib, gsum)


def _chunk_rows(rows, dk):
    return jnp.concatenate([jnp.broadcast_to(r, (GLA_CHUNK, dk)) for r in rows], axis=0)


def _gla_direction(forward, q_ref, k_ref, vt_ref, c_ref, o_ref, st_ref):
    tm = k_ref.shape[1]
    nh = GLA_HEADS
    dk = k_ref.shape[2] // nh
    dv = vt_ref.shape[1] // nh
    nc = tm // GLA_CHUNK
    half = GLA_CHUNK // 2
    ri = lax.broadcasted_iota(jnp.int32, (tm, tm), 0)
    ci = lax.broadcasted_iota(jnp.int32, (tm, tm), 1)
    same = (ri // GLA_CHUNK) == (ci // GLA_CHUNK)
    mask = same & ((ri >= ci) if forward else (ci >= ri))
    row_chunk = lax.broadcasted_iota(jnp.int32, (tm, dk), 0) // GLA_CHUNK
    order = list(range(nc)) if forward else list(reversed(range(nc)))
    zrow = jnp.zeros((1, dk), F32)
    for h in range(nh):
        ks = slice(h * dk, (h + 1) * dk)
        vs = slice(h * dv, (h + 1) * dv)
        cum = c_ref[0, :, ks]
        q = q_ref[0, :, ks].astype(F32)
        k = k_ref[0, :, ks].astype(F32)
        vt = vt_ref[0, vs, :]
        last, ref = [], []
        for c in range(nc):
            base = c * GLA_CHUNK
            li = base + GLA_CHUNK - 1 if forward else base
            ce = base + half - 1 if forward else base + half
            last.append(cum[li:li + 1])
            ref.append(cum[ce:ce + 1])
        lastb = _chunk_rows(last, dk)
        refb = _chunk_rows(ref, dk)
        qe = (q * jnp.exp(cum - refb)).astype(BF16)
        ke = (k * jnp.exp(refb - cum)).astype(BF16)
        a = jnp.where(mask, _dot_nt(qe, ke), 0.0).astype(BF16)
        o_intra = _dot_nt(a, vt)
        p_start, p_end, run = {}, {}, zrow
        for c in order:
            p_start[c] = run
            run = run + last[c]
            p_end[c] = run
        p_total = run
        qd = q * jnp.exp(cum)
        kd = k * jnp.exp(lastb - cum)
        sources = order[:-1]
        k_cols = [jnp.where(row_chunk == s, kd, 0.0).astype(BF16) for s in sources]
        k_cols.append((kd * jnp.exp(_chunk_rows([p_total - p_end[c] for c in range(nc)], dk))).astype(BF16))
        ut = _dot(vt, jnp.concatenate(k_cols, axis=1))
        st = st_ref[h]
        wt = jnp.concatenate([st, ut[:, :(nc - 1) * dk]], axis=1).astype(BF16)
        q_cols = [(qd * jnp.exp(_chunk_rows([p_start[c] for c in range(nc)], dk))).astype(BF16)]
        for i, s in enumerate(sources):
            later = order[i + 1:]
            fac = [jnp.exp(p_start[c] - p_end[s]) if c in later else zrow for c in range(nc)]
            q_cols.append((qd * _chunk_rows(fac, dk)).astype(BF16))
        o_inter = _dot_nt(jnp.concatenate(q_cols, axis=1), wt)
        st_ref[h] = st * jnp.exp(p_total) + ut[:, (nc - 1) * dk:]
        o_ref[0, :, vs] = (o_intra + o_inter).astype(o_ref.dtype)


def _gla_kernel(qf_ref, kf_ref, vtf_ref, cf_ref, qb_ref, kb_ref, vtb_ref, cb_ref,
                of_ref, ob_ref, sf_ref, sb_ref):
    @pl.when(pl.program_id(1) == 0)
    def _():
        sf_ref[...] = jnp.zeros_like(sf_ref)
        sb_ref[...] = jnp.zeros_like(sb_ref)

    _gla_direction(True, qf_ref, kf_ref, vtf_ref, cf_ref, of_ref, sf_ref)
    _gla_direction(False, qb_ref, kb_ref, vtb_ref, cb_ref, ob_ref, sb_ref)


def _gla_call(gq, gk, gvt, cumf, cumb):
    b_, l_, kw = gq.shape
    lt = gk.shape[1]
    d = gvt.shape[1]
    tm = ROW_TILE
    nt = lt // tm
    nlat = l_ // tm
    dk = kw // GLA_HEADS
    dv = d // GLA_HEADS

    bwd_blk = lambda s: jnp.where(s == 0, 0, nt - s)
    f_lat = lambda b, s: (b, jnp.maximum(s - 1, 0), 0)
    f_all = lambda b, s: (b, s, 0)
    b_lat = lambda b, s: (b, jnp.where(s == 0, nlat - 1, nlat - s), 0)
    b_all = lambda b, s: (b, bwd_blk(s), 0)
    return pl.pallas_call(
        _gla_kernel,
        grid=(b_, nt),
        in_specs=[
            pl.BlockSpec((1, tm, kw), f_lat), pl.BlockSpec((1, tm, kw), f_all),
            pl.BlockSpec((1, d, tm), lambda b, s: (b, 0, s)),
            pl.BlockSpec((1, tm, kw), f_all),
            pl.BlockSpec((1, tm, kw), b_lat), pl.BlockSpec((1, tm, kw), b_all),
            pl.BlockSpec((1, d, tm), lambda b, s: (b, 0, bwd_blk(s))),
            pl.BlockSpec((1, tm, kw), b_all),
        ],
        out_specs=[pl.BlockSpec((1, tm, d), f_lat), pl.BlockSpec((1, tm, d), b_lat)],
        out_shape=[jax.ShapeDtypeStruct((b_, l_, d), BF16), jax.ShapeDtypeStruct((b_, l_, d), BF16)],
        scratch_shapes=[pltpu.VMEM((GLA_HEADS, dv, dk), F32), pltpu.VMEM((GLA_HEADS, dv, dk), F32)],
        compiler_params=pltpu.CompilerParams(dimension_semantics=("arbitrary", "arbitrary"),
                                             vmem_limit_bytes=VMEM_LIMIT),
        name="gla",
    )(gq, gk, gvt, cumf, gq, gk, gvt, cumb)


def _attn_kernel(tk, scal_ref, q_ref, k_ref, vt_ref, o_ref, acc1, acc2, m1, l1, m2, l2):
    nsub = acc1.shape[0]
    tq = q_ref.shape[1] // nsub
    nk = k_ref.shape[1] // tk
    lam = scal_ref[0]

    def streams(sub):
        q = q_ref[0, sub * tq:(sub + 1) * tq, :]
        lane = lax.broadcasted_iota(jnp.int32, q.shape, 1)
        zero = jnp.zeros_like(q)
        q1 = jnp.where(lane < DIFF_DK, q, zero)
        q2 = jnp.where(lane >= DIFF_DK, q, zero)
        return ((q1, acc1.at[sub], m1.at[sub], l1.at[sub]), (q2, acc2.at[sub], m2.at[sub], l2.at[sub]))

    def finish(sub):
        inv1 = 1.0 / jnp.sum(l1[sub], axis=0, keepdims=True)
        inv2 = 1.0 / jnp.sum(l2[sub], axis=0, keepdims=True)
        o = acc1[sub] * inv1 - lam * (acc2[sub] * inv2)
        o = o * lax.rsqrt(jnp.mean(o * o, axis=0, keepdims=True) + EPS)
        o_ref[0, sub * tq:(sub + 1) * tq, :] = o.T.astype(o_ref.dtype)

    def bounded_tiles(sub):
        st = streams(sub)

        def scores(j):
            kk = k_ref[0, j * tk:(j + 1) * tk, :]
            return [_dot_nt(kk, qq) for qq, _, _, _ in st]

        s_next = scores(0)
        for j in range(nk):
            s_cur = s_next
            if j + 1 < nk:
                s_next = scores(j + 1)
            vt = vt_ref[0, :, j * tk:(j + 1) * tk]
            for s, (_, acc, _, l) in zip(s_cur, st):
                p = jnp.exp2(s)
                l_part = jnp.sum(p.reshape(tk // SUBLANES, SUBLANES, tq), axis=0)
                pv = _dot(vt, p.astype(BF16))
                l[...] = l_part if j == 0 else l[...] + l_part
                acc[...] = pv if j == 0 else acc[...] + pv

    def online_tiles(sub):
        st = streams(sub)
        for _, acc, m, l in st:
            acc[...] = jnp.zeros_like(acc)
            l[...] = jnp.zeros_like(l)
            m[...] = jnp.full_like(m, -1e30)

        def body(j, carry):
            off = pl.multiple_of(j * tk, tk)
            kk = k_ref[0, pl.ds(off, tk), :]
            vt = vt_ref[0, :, pl.ds(off, tk)]
            for qq, acc, m, l in st:
                s = _dot_nt(kk, qq)
                m_new = jnp.maximum(m[...], jnp.max(s, axis=0, keepdims=True))
                alpha = jnp.exp2(m[...] - m_new)
                p = jnp.exp2(s - m_new)
                l[0:1] = alpha * l[0:1] + jnp.sum(p, axis=0, keepdims=True)
                acc[...] = alpha * acc[...] + _dot(vt, p.astype(BF16))
                m[...] = m_new
            return carry

        lax.fori_loop(0, nk, body, 0)

    bounded = scal_ref[1] != 0.0

    @pl.when(bounded)
    def _():
        for sub in range(nsub):
            bounded_tiles(sub)
            finish(sub)

    @pl.when(jnp.logical_not(bounded))
    def _():
        for sub in range(nsub):
            online_tiles(sub)
            finish(sub)


def _attn_call(scal, dq, dk, dvt, tq, tk, nsub):
    b_, l_, d = dq.shape
    lt = dk.shape[1]
    nh = d // LANES
    tstep = tq * nsub
    assert l_ % tstep == 0 and lt % tk == 0
    return pl.pallas_call(
        functools.partial(_attn_kernel, tk),
        grid=(b_, nh, l_ // tstep),
        in_specs=[
            pl.BlockSpec(memory_space=pltpu.SMEM),
            pl.BlockSpec((1, tstep, LANES), lambda b, h, i: (b, i, h)),
            pl.BlockSpec((1, lt, LANES), lambda b, h, i: (b, 0, h)),
            pl.BlockSpec((1, LANES, lt), lambda b, h, i: (b, h, 0)),
        ],
        out_specs=pl.BlockSpec((1, tstep, LANES), lambda b, h, i: (b, i, h)),
        out_shape=jax.ShapeDtypeStruct((b_, l_, d), BF16),
        scratch_shapes=[pltpu.VMEM((nsub, LANES, tq), F32), pltpu.VMEM((nsub, LANES, tq), F32),
                        pltpu.VMEM((nsub, 1, tq), F32), pltpu.VMEM((nsub, SUBLANES, tq), F32),
                        pltpu.VMEM((nsub, 1, tq), F32), pltpu.VMEM((nsub, SUBLANES, tq), F32)],
        compiler_params=pltpu.CompilerParams(
            dimension_semantics=("arbitrary", "arbitrary", "arbitrary"),
            vmem_limit_bytes=VMEM_LIMIT),
        name="diffattn",
    )(scal, dq, dk, dvt)


def _combine_kernel(x_ref, g_ref, of_ref, ob_ref, od_ref, gg_ref, dg_ref, mg_ref, md_ref,
                    ggain_ref, dgain_ref, wg_ref, wd_ref, wo_ref, out_ref):
    d = x_ref.shape[2]
    dv = d // GLA_HEADS
    og = of_ref[0].astype(F32) + ob_ref[0].astype(F32)
    normed = []
    for h in range(GLA_HEADS):
        sl = og[:, h * dv:(h + 1) * dv]
        normed.append(sl * lax.rsqrt(jnp.mean(sl * sl, axis=-1, keepdims=True) + EPS))
    a = jnp.concatenate(normed, axis=1) * ggain_ref[...] * gg_ref[0].astype(F32)
    bb = od_ref[0].astype(F32) * dgain_ref[...] * dg_ref[0].astype(F32)
    y = (mg_ref[0].astype(F32) * _dot(a.astype(BF16), wg_ref[...])
         + md_ref[0].astype(F32) * _dot(bb.astype(BF16), wd_ref[...]))
    out_ref[0] = x_ref[0] + g_ref[0] * _dot(y.astype(BF16), wo_ref[...])


def _combine_call(x, mod3, o_f, o_b, o_d, gg, dg, mg, md, ggain, dgain, wg, wd, wo, tm):
    b_, l_, d = x.shape
    row = lambda b, i: (b, i, 0)
    const2 = lambda b, i: (0, 0)
    once = pl.Buffered(1)
    big = pl.BlockSpec((1, tm, d), row)
    wspec = pl.BlockSpec((d, d), const2, pipeline_mode=once)
    return pl.pallas_call(
        _combine_kernel,
        grid=(b_, l_ // tm),
        in_specs=[big, pl.BlockSpec((1, 1, d), lambda b, i: (b, 0, 2)),
                  big, big, big, big, big, big, big,
                  pl.BlockSpec((1, d), const2, pipeline_mode=once),
                  pl.BlockSpec((1, d), const2, pipeline_mode=once),
                  wspec, wspec, wspec],
        out_specs=big,
        out_shape=jax.ShapeDtypeStruct(x.shape, x.dtype),
        compiler_params=pltpu.CompilerParams(dimension_semantics=("arbitrary", "arbitrary"),
                                             vmem_limit_bytes=VMEM_LIMIT),
        name="combine",
    )(x, mod3, o_f, o_b, o_d, gg, dg, mg, md, ggain, dgain, wg, wd, wo)


def _rope_tables(n_tokens, n_ctx):
    f32 = np.float32
    rows = n_tokens // GRID_W
    inv_freq = f32(ROPE_BASE) ** (-np.arange(0, ROPE_AXIS_DIM, 2, dtype=f32) / f32(ROPE_AXIS_DIM))
    ang_r = (np.arange(rows, dtype=f32)[:, None] * inv_freq).astype(f32)
    ang_c = (np.arange(GRID_W, dtype=f32)[:, None] * inv_freq).astype(f32)
    sign = np.where(np.arange(ROPE_AXIS_DIM) < ROPE_AXIS_DIM // 2, -1.0, 1.0)
    two = lambda t: np.concatenate([t, t], axis=-1).astype(f32)

    def table(fn, sgn):
        tr = jnp.broadcast_to(jnp.asarray(two(fn(ang_r)) * sgn, F32)[:, None, :], (rows, GRID_W, ROPE_AXIS_DIM))
        tc = jnp.broadcast_to(jnp.asarray(two(fn(ang_c)) * sgn, F32)[None, :, :], (rows, GRID_W, ROPE_AXIS_DIM))
        return jnp.concatenate([tr, tc, tr, tc], axis=-1).reshape(n_tokens, LANES)

    cos = jnp.concatenate([jnp.ones((n_ctx, LANES), F32), table(np.cos, 1.0)], axis=0)
    sin = jnp.concatenate([jnp.zeros((n_ctx, LANES), F32), table(np.sin, sign)], axis=0)
    return cos, sin


def _block_tri(n, upper):
    i = np.arange(n)
    same = (i[:, None] // GLA_CHUNK) == (i[None, :] // GLA_CHUNK)
    tri = (i[None, :] >= i[:, None]) if upper else (i[:, None] >= i[None, :])
    return jnp.asarray((same & tri).astype(np.float32)).astype(BF16)


def _attn_tiles(l_, lt):
    tq = 1024 if l_ % 1024 == 0 else l_
    nsub = 2 if l_ % (2 * tq) == 0 else 1
    for tk in (768, 512, 384, 256, 128):
        if lt % tk == 0:
            return tq, tk, nsub
    raise ValueError("key length must be a multiple of 128")


def kernel(x, c, ctx, c_ctx, w_ada, b_ada, w_in, gla_w_decay, gla_b_decay, gla_norm, diff_q_norm,
           diff_k_norm, diff_lambda, diff_norm, w_br_gla, w_br_diff, w_out):
    b_, l_, d = x.shape
    n_ctx = ctx.shape[1]
    depth = w_ada.shape[0]
    assert depth == 1, "single-layer block"
    layer = 0
    lam_init = 0.8 - 0.6 * math.exp(-0.3 * layer)
    half_k = d // 2

    n_mod = -(-(b_ + 1) // SUBLANES) * SUBLANES
    cc = jnp.zeros((n_mod, d), F32).at[:b_].set(c).at[b_].set(c_ctx)
    mod, lam_tile = _mod_call(cc, w_ada[layer], b_ada[layer][None, :], diff_lambda[layer], lam_init)
    mod3 = mod.reshape(n_mod, 1, 3 * d)
    lam = lam_tile[0, :1]

    w = w_in[layer]
    sizes = (half_k, half_k, d, d, 2 * GLA_RANK, d, d, d, d, d, d)
    names = ("gq", "gk", "gv", "gg", "lr", "dq", "dk", "dv", "dg", "mg", "md")
    starts = np.concatenate([[0], np.cumsum(sizes)])
    lr_at = names.index("lr")
    lr_lo, lr_hi = int(starts[lr_at]), int(starts[lr_at + 1])
    col_scale = np.ones((1, int(starts[-1])), np.float32)
    for name in ("gg", "dg", "mg", "md"):
        i = names.index(name)
        col_scale[:, int(starts[i]):int(starts[i + 1])] = 0.5
    col_scale[:, int(starts[0]):int(starts[1])] = (half_k // GLA_HEADS) ** -0.5
    w_a = (w[:, :lr_lo] * col_scale[:, :lr_lo]).astype(BF16)
    w_b = (w[:, lr_hi:] * col_scale[:, lr_hi:]).astype(BF16)
    w_lr = jnp.concatenate([w[:, lr_lo:lr_hi]] * 3 + [jnp.zeros((d, LANES - 6 * GLA_RANK), w.dtype)],
                           axis=1).astype(BF16)
    offs = {"lr": (2, 0, LANES)}
    for i, name in enumerate(names):
        if i < lr_at:
            offs[name] = (0, int(starts[i]), int(starts[i + 1]))
        elif i > lr_at:
            offs[name] = (1, int(starts[i]) - lr_hi, int(starts[i + 1]) - lr_hi)

    zk = jnp.zeros((GLA_RANK, half_k), F32)
    wd_f32 = jnp.concatenate([jnp.concatenate([gla_w_decay[layer, 0], zk], axis=1),
                              jnp.concatenate([zk, gla_w_decay[layer, 1]], axis=1)], axis=0)
    bd_f32 = jnp.concatenate([gla_b_decay[layer, 0], gla_b_decay[layer, 1]])[None, :]
    hi_lo = lambda t: (t.astype(BF16), (t - t.astype(BF16).astype(F32)).astype(BF16))
    (wd_hi, wd_lo), (bd_hi, bd_lo) = hi_lo(wd_f32), hi_lo(bd_f32)
    wdec = jnp.concatenate([wd_hi, wd_lo, wd_hi, bd_hi, bd_lo,
                            jnp.zeros((LANES - 6 * GLA_RANK - 2, 2 * half_k), BF16)], axis=0)

    cos, sin_signed = _rope_tables(l_, n_ctx)
    lane = np.arange(LANES)
    partner = np.where(lane % ROPE_AXIS_DIM < ROPE_AXIS_DIM // 2,
                       lane + ROPE_AXIS_DIM // 2, lane - ROPE_AXIS_DIM // 2)
    g_q = jnp.tile(diff_q_norm[layer], LANES // DIFF_DK) * (DIFF_DK ** -0.5 * LOG2E)
    g_k = jnp.tile(diff_k_norm[layer], LANES // DIFF_DK)
    gain_rows = [g_q, g_q[partner], g_k, g_k[partner]]
    gains = jnp.stack(gain_rows + [jnp.zeros((LANES,), F32)] * (SUBLANES - len(gain_rows)))

    trif = _block_tri(ROW_TILE, upper=False) * (1.0 / GLA_TAU)
    trib = _block_tri(ROW_TILE, upper=True) * (1.0 / GLA_TAU)
    gi = np.arange(MXU_TILE)
    gsum = jnp.asarray((gi[:, None] // DIFF_DK == gi[None, :] // DIFF_DK).astype(np.float32)
                       / DIFF_DK).astype(BF16)

    (gq, gk, gvt, cumf, cumb, gg, dg, mg, md, dq, dk, dvt) = _proj_call(
        x, ctx, mod3, w_a, w_b, w_lr, offs, wdec, gains, cos, sin_signed, trif, trib, gsum)

    o_f, o_b = _gla_call(gq, gk, gvt, cumf, cumb)

    tq, tk, nsub = _attn_tiles(l_, l_ + n_ctx)
    score_bound = DIFF_DK ** 0.5 * jnp.max(jnp.abs(diff_q_norm[layer])) * jnp.max(jnp.abs(diff_k_norm[layer]))
    bounded = (score_bound * BOUND_MARGIN <= SCORE_BOUND).astype(F32)
    scal = jnp.concatenate([lam, bounded[None]])
    o_d = _attn_call(scal, dq, dk, dvt, tq, tk, nsub)

    ggain = jnp.tile(gla_norm[layer], GLA_HEADS)[None, :]
    dgain = (jnp.tile(diff_norm[layer], d // diff_norm.shape[1]) * (1.0 - lam_init))[None, :]
    tm = 512 if l_ % 512 == 0 else l_
    return _combine_call(x, mod3, o_f, o_b, o_d, gg, dg, mg, md, ggain, dgain,
                         w_br_gla[layer].astype(BF16), w_br_diff[layer].astype(BF16),
                         w_out[layer].astype(BF16), tm)
```

```python
import functools
import math

import jax
import jax.numpy as jnp
import numpy as np
from jax import lax
from jax.experimental import pallas as pl
from jax.experimental.pallas import tpu as pltpu

F32 = jnp.float32
BF16 = jnp.bfloat16
HIGHEST = lax.Precision.HIGHEST

EPS = 1e-6
GRID_W = 64
GLA_HEADS = 4
GLA_RANK = 16
GLA_TAU = 16.0
GLA_CHUNK = 64
DIFF_DK = 64
ROPE_BASE = 10000.0
ROPE_AXIS_DIM = DIFF_DK // 2
LANES = 128
SUBLANES = 8
MXU_TILE = 256
ROW_TILE = 256
VMEM_LIMIT = 56 * 1024 * 1024
LOG2E = math.log2(math.e)
SCORE_BOUND = 40.0
BOUND_MARGIN = 1.02


def _dot(a, b):
    return jnp.dot(a, b, preferred_element_type=F32)


def _dot_nt(a, b):
    return lax.dot_general(a, b, (((1,), (1,)), ((), ())), preferred_element_type=F32)


def _dot_exact(a, b):
    return lax.dot_general(a, b, (((1,), (0,)), ((), ())), precision=HIGHEST,
                           preferred_element_type=F32)


def _split_bf16(a):
    hi = a.astype(BF16)
    return hi, (a - hi.astype(F32)).astype(BF16)


def _sigmoid(t):
    return 0.5 * jnp.tanh(0.5 * t) + 0.5


def _silu(t):
    return t * _sigmoid(t)


def _mod_kernel(lam_init, cc_ref, w_ref, b_ref, lamp_ref, mod_ref, lam_ref):
    cc = cc_ref[...]
    mod_ref[...] = _dot_exact(_silu(cc), w_ref[...]) + b_ref[...]
    lp = lamp_ref[...]
    e1 = jnp.exp(jnp.sum(lp[0:1] * lp[1:2], axis=-1, keepdims=True))
    e2 = jnp.exp(jnp.sum(lp[2:3] * lp[3:4], axis=-1, keepdims=True))
    lam_ref[...] = jnp.broadcast_to(e1 - e2 + lam_init, lam_ref.shape)


def _mod_call(cc, w_ada, b_ada, lam_params, lam_init):
    rows, d = cc.shape
    n = w_ada.shape[1]
    return pl.pallas_call(
        functools.partial(_mod_kernel, lam_init),
        grid=(n // d,),
        in_specs=[
            pl.BlockSpec((rows, d), lambda j: (0, 0)),
            pl.BlockSpec((d, d), lambda j: (0, j)),
            pl.BlockSpec((1, d), lambda j: (0, j)),
            pl.BlockSpec(lam_params.shape, lambda j: (0, 0)),
        ],
        out_specs=[
            pl.BlockSpec((rows, d), lambda j: (0, j)),
            pl.BlockSpec((rows, LANES), lambda j: (0, 0)),
        ],
        out_shape=[jax.ShapeDtypeStruct((rows, n), F32), jax.ShapeDtypeStruct((rows, LANES), F32)],
        compiler_params=pltpu.CompilerParams(dimension_semantics=("arbitrary",)),
        name="mod",
    )(cc, w_ada, b_ada, lam_params)


def _group_norm_rope(p, gsum_ref, cos_g, sin_g, lo_mask):
    tm, d = p.shape
    sq = (p * p).astype(BF16)
    slabs = []
    for s in range(d // MXU_TILE):
        slabs.append(_dot(sq[:, s * MXU_TILE:(s + 1) * MXU_TILE], gsum_ref[...]))
    n = p * lax.rsqrt(jnp.concatenate(slabs, axis=1) + EPS)
    outs = []
    for h in range(d // LANES):
        sl = n[:, h * LANES:(h + 1) * LANES]
        up = pltpu.roll(sl, LANES - ROPE_AXIS_DIM // 2, axis=1)
        dn = pltpu.roll(sl, ROPE_AXIS_DIM // 2, axis=1)
        outs.append(sl * cos_g + jnp.where(lo_mask, up, dn) * sin_g)
    return jnp.concatenate(outs, axis=1)


def _proj_kernel(offs, x_ref, ctx_ref, shift_ref, scale_ref, wa_ref, wb_ref, wlr_ref, wdec_ref,
                 gains_ref, cos_ref, sin_ref, trif_ref, trib_ref, gsum_ref,
                 gq_ref, gk_ref, gvt_ref, cf_ref, cb_ref, gg_ref, dg_ref, mg_ref, md_ref,
                 dq_ref, dk_ref, dvt_ref):
    t = pl.program_id(1)
    latent = t > 0
    xin = jnp.where(latent, x_ref[0], ctx_ref[0])
    ms = jnp.mean(xin * xin, axis=-1, keepdims=True)
    h = xin * lax.rsqrt(ms + EPS) * (1.0 + scale_ref[0]) + shift_ref[0]
    hb = h.astype(BF16)

    def proj(name):
        which, a, b = offs[name]
        return _dot(hb, (wa_ref, wb_ref, wlr_ref)[which][:, a:b])

    lr3 = proj("lr")
    lr_hi, lr_lo = _split_bf16(lr3)
    lane = lax.broadcasted_iota(jnp.int32, lr3.shape, 1)
    lhs = jnp.where(lane < 4 * GLA_RANK, lr_hi,
                    jnp.where(lane < 6 * GLA_RANK, lr_lo, jnp.ones_like(lr_lo)))
    z = _dot(lhs, wdec_ref[...])
    la = jnp.minimum(z, 0.0) - jnp.log(1.0 + jnp.exp(-jnp.abs(z)))
    half = la.shape[1] // 2
    la_hi, la_lo = _split_bf16(la)

    gq_ref[0] = (proj("gq") * (gq_ref.shape[2] // GLA_HEADS) ** -0.5).astype(BF16)
    gk_ref[0] = proj("gk").astype(BF16)
    gvt_ref[0] = proj("gv").T.astype(BF16)

    for name, ref in (("gg", gg_ref), ("dg", dg_ref)):
        hg = proj(name)
        ref[0] = (hg * jnp.tanh(hg) + hg).astype(BF16)
    for name, ref in (("mg", mg_ref), ("md", md_ref)):
        ref[0] = (0.5 * jnp.tanh(proj(name)) + 0.5).astype(BF16)

    cf_ref[0] = _dot(trif_ref[...], la_hi[:, :half]) + _dot(trif_ref[...], la_lo[:, :half])
    cb_ref[0] = _dot(trib_ref[...], la_hi[:, half:]) + _dot(trib_ref[...], la_lo[:, half:])

    cos = cos_ref[...]
    sin_signed = sin_ref[...]
    lane = lax.broadcasted_iota(jnp.int32, cos.shape, 1)
    lo_mask = (lane % ROPE_AXIS_DIM) < (ROPE_AXIS_DIM // 2)
    dq_ref[0] = _group_norm_rope(proj("dq"), gsum_ref, cos * gains_ref[0:1], sin_signed * gains_ref[1:2],
                                 lo_mask).astype(BF16)
    dk_ref[0] = _group_norm_rope(proj("dk"), gsum_ref, cos * gains_ref[2:3], sin_signed * gains_ref[3:4],
                                 lo_mask).astype(BF16)
    dvt_ref[0] = proj("dv").T.astype(BF16)


def _proj_call(x, ctx, mod3, w_a, w_b, w_lr, offs, wdec, gains, cos, sin_signed,
               trif, trib, gsum):
    b_, l_, d = x.shape
    tm = ROW_TILE
    assert ctx.shape[1] == tm and l_ % tm == 0
    nt = l_ // tm + 1
    lt = l_ + tm
    n_ctx_row = b_
    half_k = offs["gq"][2] - offs["gq"][1]

    lat = lambda b, t: (b, jnp.maximum(t - 1, 0), 0)
    allr = lambda b, t: (b, t, 0)
    const2 = lambda b, t: (0, 0)
    once = pl.Buffered(1)

    def mod_spec(col):
        return pl.BlockSpec((1, 1, d), lambda b, t: (jnp.where(t == 0, n_ctx_row, b), 0, col))

    in_specs = [
        pl.BlockSpec((1, tm, d), lat),
        pl.BlockSpec((1, tm, d), lambda b, t: (b, 0, 0)),
        mod_spec(0), mod_spec(1),
        pl.BlockSpec(w_a.shape, const2, pipeline_mode=once),
        pl.BlockSpec(w_b.shape, const2, pipeline_mode=once),
        pl.BlockSpec(w_lr.shape, const2, pipeline_mode=once),
        pl.BlockSpec(wdec.shape, const2, pipeline_mode=once),
        pl.BlockSpec(gains.shape, const2, pipeline_mode=once),
        pl.BlockSpec((tm, LANES), lambda b, t: (t, 0)),
        pl.BlockSpec((tm, LANES), lambda b, t: (t, 0)),
        pl.BlockSpec(trif.shape, const2, pipeline_mode=once),
        pl.BlockSpec(trib.shape, const2, pipeline_mode=once),
        pl.BlockSpec(gsum.shape, const2, pipeline_mode=once),
    ]
    out_specs = [
        pl.BlockSpec((1, tm, half_k), lat),
        pl.BlockSpec((1, tm, half_k), allr),
        pl.BlockSpec((1, d, tm), lambda b, t: (b, 0, t)),
        pl.BlockSpec((1, tm, half_k), allr),
        pl.BlockSpec((1, tm, half_k), allr),
        pl.BlockSpec((1, tm, d), lat),
        pl.BlockSpec((1, tm, d), lat),
        pl.BlockSpec((1, tm, d), lat),
        pl.BlockSpec((1, tm, d), lat),
        pl.BlockSpec((1, tm, d), lat),
        pl.BlockSpec((1, tm, d), allr),
        pl.BlockSpec((1, d, tm), lambda b, t: (b, 0, t)),
    ]
    sd = jax.ShapeDtypeStruct
    out_shape = [
        sd((b_, l_, half_k), BF16), sd((b_, lt, half_k), BF16), sd((b_, d, lt), BF16),
        sd((b_, lt, half_k), F32), sd((b_, lt, half_k), F32),
        sd((b_, l_, d), BF16), sd((b_, l_, d), BF16), sd((b_, l_, d), BF16), sd((b_, l_, d), BF16),
        sd((b_, l_, d), BF16), sd((b_, lt, d), BF16), sd((b_, d, lt), BF16),
    ]
    return pl.pallas_call(
        functools.partial(_proj_kernel, offs),
        grid=(b_, nt),
        in_specs=in_specs, out_specs=out_specs, out_shape=out_shape,
        compiler_params=pltpu.CompilerParams(dimension_semantics=("arbitrary", "arbitrary"),
                                             vmem_limit_bytes=VMEM_LIMIT),
        name="proj",
    )(x, ctx, mod3, mod3, w_a, w_b, w_lr, wdec, gains, cos, sin_signed, trif, trib, gsum)


def _chunk_rows(rows, dk):
    return jnp.concatenate([jnp.broadcast_to(r, (GLA_CHUNK, dk)) for r in rows], axis=0)


def _gla_direction(forward, q_ref, k_ref, vt_ref, c_ref, o_ref, st_ref):
    tm = k_ref.shape[1]
    nh = GLA_HEADS
    dk = k_ref.shape[2] // nh
    dv = vt_ref.shape[1] // nh
    nc = tm // GLA_CHUNK
    half = GLA_CHUNK // 2
    ri = lax.broadcasted_iota(jnp.int32, (tm, tm), 0)
    ci = lax.broadcasted_iota(jnp.int32, (tm, tm), 1)
    same = (ri // GLA_CHUNK) == (ci // GLA_CHUNK)
    mask = same & ((ri >= ci) if forward else (ci >= ri))
    row_chunk = lax.broadcasted_iota(jnp.int32, (tm, dk), 0) // GLA_CHUNK
    order = list(range(nc)) if forward else list(reversed(range(nc)))
    zrow = jnp.zeros((1, dk), F32)
    for h in range(nh):
        ks = slice(h * dk, (h + 1) * dk)
        vs = slice(h * dv, (h + 1) * dv)
        cum = c_ref[0, :, ks]
        q = q_ref[0, :, ks].astype(F32)
        k = k_ref[0, :, ks].astype(F32)
        vt = vt_ref[0, vs, :]
        last, ref = [], []
        for c in range(nc):
            base = c * GLA_CHUNK
            li = base + GLA_CHUNK - 1 if forward else base
            ce = base + half - 1 if forward else base + half
            last.append(cum[li:li + 1])
            ref.append(cum[ce:ce + 1])
        lastb = _chunk_rows(last, dk)
        refb = _chunk_rows(ref, dk)
        qe = (q * jnp.exp(cum - refb)).astype(BF16)
        ke = (k * jnp.exp(refb - cum)).astype(BF16)
        a = jnp.where(mask, _dot_nt(qe, ke), 0.0).astype(BF16)
        o_intra = _dot_nt(a, vt)
        p_start, p_end, run = {}, {}, zrow
        for c in order:
            p_start[c] = run
            run = run + last[c]
            p_end[c] = run
        p_total = run
        qd = q * jnp.exp(cum)
        kd = k * jnp.exp(lastb - cum)
        sources = order[:-1]
        k_cols = [jnp.where(row_chunk == s, kd, 0.0).astype(BF16) for s in sources]
        k_cols.append((kd * jnp.exp(_chunk_rows([p_total - p_end[c] for c in range(nc)], dk))).astype(BF16))
        ut = _dot(vt, jnp.concatenate(k_cols, axis=1))
        st = st_ref[h]
        wt = jnp.concatenate([st, ut[:, :(nc - 1) * dk]], axis=1).astype(BF16)
        q_cols = [(qd * jnp.exp(_chunk_rows([p_start[c] for c in range(nc)], dk))).astype(BF16)]
        for i, s in enumerate(sources):
            later = order[i + 1:]
            fac = [jnp.exp(p_start[c] - p_end[s]) if c in later else zrow for c in range(nc)]
            q_cols.append((qd * _chunk_rows(fac, dk)).astype(BF16))
        o_inter = _dot_nt(jnp.concatenate(q_cols, axis=1), wt)
        st_ref[h] = st * jnp.exp(p_total) + ut[:, (nc - 1) * dk:]
        o_ref[0, :, vs] = (o_intra + o_inter).astype(o_ref.dtype)


def _gla_kernel(qf_ref, kf_ref, vtf_ref, cf_ref, qb_ref, kb_ref, vtb_ref, cb_ref,
                of_ref, ob_ref, sf_ref, sb_ref):
    @pl.when(pl.program_id(1) == 0)
    def _():
        sf_ref[...] = jnp.zeros_like(sf_ref)
        sb_ref[...] = jnp.zeros_like(sb_ref)

    _gla_direction(True, qf_ref, kf_ref, vtf_ref, cf_ref, of_ref, sf_ref)
    _gla_direction(False, qb_ref, kb_ref, vtb_ref, cb_ref, ob_ref, sb_ref)


def _gla_call(gq, gk, gvt, cumf, cumb):
    b_, l_, kw = gq.shape
    lt = gk.shape[1]
    d = gvt.shape[1]
    tm = ROW_TILE
    nt = lt // tm
    nlat = l_ // tm
    dk = kw // GLA_HEADS
    dv = d // GLA_HEADS

    bwd_blk = lambda s: jnp.where(s == 0, 0, nt - s)
    f_lat = lambda b, s: (b, jnp.maximum(s - 1, 0), 0)
    f_all = lambda b, s: (b, s, 0)
    b_lat = lambda b, s: (b, jnp.where(s == 0, nlat - 1, nlat - s), 0)
    b_all = lambda b, s: (b, bwd_blk(s), 0)
    return pl.pallas_call(
        _gla_kernel,
        grid=(b_, nt),
        in_specs=[
            pl.BlockSpec((1, tm, kw), f_lat), pl.BlockSpec((1, tm, kw), f_all),
            pl.BlockSpec((1, d, tm), lambda b, s: (b, 0, s)),
            pl.BlockSpec((1, tm, kw), f_all),
            pl.BlockSpec((1, tm, kw), b_lat), pl.BlockSpec((1, tm, kw), b_all),
            pl.BlockSpec((1, d, tm), lambda b, s: (b, 0, bwd_blk(s))),
            pl.BlockSpec((1, tm, kw), b_all),
        ],
        out_specs=[pl.BlockSpec((1, tm, d), f_lat), pl.BlockSpec((1, tm, d), b_lat)],
        out_shape=[jax.ShapeDtypeStruct((b_, l_, d), BF16), jax.ShapeDtypeStruct((b_, l_, d), BF16)],
        scratch_shapes=[pltpu.VMEM((GLA_HEADS, dv, dk), F32), pltpu.VMEM((GLA_HEADS, dv, dk), F32)],
        compiler_params=pltpu.CompilerParams(dimension_semantics=("arbitrary", "arbitrary"),
                                             vmem_limit_bytes=VMEM_LIMIT),
        name="gla",
    )(gq, gk, gvt, cumf, gq, gk, gvt, cumb)


def _attn_kernel(tk, scal_ref, q_ref, k_ref, vt_ref, o_ref, acc1, acc2, m1, l1, m2, l2):
    nsub = acc1.shape[0]
    tq = q_ref.shape[1] // nsub
    nk = k_ref.shape[1] // tk
    lam = scal_ref[0]

    def streams(sub):
        q = q_ref[0, sub * tq:(sub + 1) * tq, :]
        lane = lax.broadcasted_iota(jnp.int32, q.shape, 1)
        zero = jnp.zeros_like(q)
        q1 = jnp.where(lane < DIFF_DK, q, zero)
        q2 = jnp.where(lane >= DIFF_DK, q, zero)
        return ((q1, acc1.at[sub], m1.at[sub], l1.at[sub]), (q2, acc2.at[sub], m2.at[sub], l2.at[sub]))

    def finish(sub):
        inv1 = 1.0 / jnp.sum(l1[sub], axis=0, keepdims=True)
        inv2 = 1.0 / jnp.sum(l2[sub], axis=0, keepdims=True)
        o = acc1[sub] * inv1 - lam * (acc2[sub] * inv2)
        o = o * lax.rsqrt(jnp.mean(o * o, axis=0, keepdims=True) + EPS)
        o_ref[0, sub * tq:(sub + 1) * tq, :] = o.T.astype(o_ref.dtype)

    def bounded_tiles(sub):
        st = streams(sub)

        def scores(j):
            kk = k_ref[0, j * tk:(j + 1) * tk, :]
            return [_dot_nt(kk, qq) for qq, _, _, _ in st]

        s_next = scores(0)
        for j in range(nk):
            s_cur = s_next
            if j + 1 < nk:
                s_next = scores(j + 1)
            vt = vt_ref[0, :, j * tk:(j + 1) * tk]
            for s, (_, acc, _, l) in zip(s_cur, st):
                p = jnp.exp2(s)
                l_part = jnp.sum(p.reshape(tk // SUBLANES, SUBLANES, tq), axis=0)
                pv = _dot(vt, p.astype(BF16))
                l[...] = l_part if j == 0 else l[...] + l_part
                acc[...] = pv if j == 0 else acc[...] + pv

    def online_tiles(sub):
        st = streams(sub)
        for _, acc, m, l in st:
            acc[...] = jnp.zeros_like(acc)
            l[...] = jnp.zeros_like(l)
            m[...] = jnp.full_like(m, -1e30)

        def body(j, carry):
            off = pl.multiple_of(j * tk, tk)
            kk = k_ref[0, pl.ds(off, tk), :]
            vt = vt_ref[0, :, pl.ds(off, tk)]
            for qq, acc, m, l in st:
                s = _dot_nt(kk, qq)
                m_new = jnp.maximum(m[...], jnp.max(s, axis=0, keepdims=True))
                alpha = jnp.exp2(m[...] - m_new)
                p = jnp.exp2(s - m_new)
                l[0:1] = alpha * l[0:1] + jnp.sum(p, axis=0, keepdims=True)
                acc[...] = alpha * acc[...] + _dot(vt, p.astype(BF16))
                m[...] = m_new
            return carry

        lax.fori_loop(0, nk, body, 0)

    bounded = scal_ref[1] != 0.0

    @pl.when(bounded)
    def _():
        for sub in range(nsub):
            bounded_tiles(sub)
            finish(sub)

    @pl.when(jnp.logical_not(bounded))
    def _():
        for sub in range(nsub):
            online_tiles(sub)
            finish(sub)


def _attn_call(scal, dq, dk, dvt, tq, tk, nsub):
    b_, l_, d = dq.shape
    lt = dk.shape[1]
    nh = d // LANES
    tstep = tq * nsub
    assert l_ % tstep == 0 and lt % tk == 0
    return pl.pallas_call(
        functools.partial(_attn_kernel, tk),
        grid=(b_, nh, l_ // tstep),
        in_specs=[
            pl.BlockSpec(memory_space=pltpu.SMEM),
            pl.BlockSpec((1, tstep, LANES), lambda b, h, i: (b, i, h)),
            pl.BlockSpec((1, lt, LANES), lambda b, h, i: (b, 0, h)),
            pl.BlockSpec((1, LANES, lt), lambda b, h, i: (b, h, 0)),
        ],
        out_specs=pl.BlockSpec((1, tstep, LANES), lambda b, h, i: (b, i, h)),
        out_shape=jax.ShapeDtypeStruct((b_, l_, d), BF16),
        scratch_shapes=[pltpu.VMEM((nsub, LANES, tq), F32), pltpu.VMEM((nsub, LANES, tq), F32),
                        pltpu.VMEM((nsub, 1, tq), F32), pltpu.VMEM((nsub, SUBLANES, tq), F32),
                        pltpu.VMEM((nsub, 1, tq), F32), pltpu.VMEM((nsub, SUBLANES, tq), F32)],
        compiler_params=pltpu.CompilerParams(
            dimension_semantics=("arbitrary", "arbitrary", "arbitrary"),
            vmem_limit_bytes=VMEM_LIMIT),
        name="diffattn",
    )(scal, dq, dk, dvt)


def _combine_kernel(x_ref, g_ref, of_ref, ob_ref, od_ref, gg_ref, dg_ref, mg_ref, md_ref,
                    ggain_ref, dgain_ref, wg_ref, wd_ref, wo_ref, out_ref):
    d = x_ref.shape[2]
    dv = d // GLA_HEADS
    og = of_ref[0].astype(F32) + ob_ref[0].astype(F32)
    normed = []
    for h in range(GLA_HEADS):
        sl = og[:, h * dv:(h + 1) * dv]
        normed.append(sl * lax.rsqrt(jnp.mean(sl * sl, axis=-1, keepdims=True) + EPS))
    a = jnp.concatenate(normed, axis=1) * ggain_ref[...] * gg_ref[0].astype(F32)
    bb = od_ref[0].astype(F32) * dgain_ref[...] * dg_ref[0].astype(F32)
    y = (mg_ref[0].astype(F32) * _dot(a.astype(BF16), wg_ref[...])
         + md_ref[0].astype(F32) * _dot(bb.astype(BF16), wd_ref[...]))
    out_ref[0] = x_ref[0] + g_ref[0] * _dot(y.astype(BF16), wo_ref[...])


def _combine_call(x, mod3, o_f, o_b, o_d, gg, dg, mg, md, ggain, dgain, wg, wd, wo, tm):
    b_, l_, d = x.shape
    row = lambda b, i: (b, i, 0)
    const2 = lambda b, i: (0, 0)
    once = pl.Buffered(1)
    big = pl.BlockSpec((1, tm, d), row)
    wspec = pl.BlockSpec((d, d), const2, pipeline_mode=once)
    return pl.pallas_call(
        _combine_kernel,
        grid=(b_, l_ // tm),
        in_specs=[big, pl.BlockSpec((1, 1, d), lambda b, i: (b, 0, 2)),
                  big, big, big, big, big, big, big,
                  pl.BlockSpec((1, d), const2, pipeline_mode=once),
                  pl.BlockSpec((1, d), const2, pipeline_mode=once),
                  wspec, wspec, wspec],
        out_specs=big,
        out_shape=jax.ShapeDtypeStruct(x.shape, x.dtype),
        compiler_params=pltpu.CompilerParams(dimension_semantics=("arbitrary", "arbitrary"),
                                             vmem_limit_bytes=VMEM_LIMIT),
        name="combine",
    )(x, mod3, o_f, o_b, o_d, gg, dg, mg, md, ggain, dgain, wg, wd, wo)


def _rope_tables(n_tokens, n_ctx):
    f32 = np.float32
    rows = n_tokens // GRID_W
    inv_freq = f32(ROPE_BASE) ** (-np.arange(0, ROPE_AXIS_DIM, 2, dtype=f32) / f32(ROPE_AXIS_DIM))
    ang_r = (np.arange(rows, dtype=f32)[:, None] * inv_freq).astype(f32)
    ang_c = (np.arange(GRID_W, dtype=f32)[:, None] * inv_freq).astype(f32)
    sign = np.where(np.arange(ROPE_AXIS_DIM) < ROPE_AXIS_DIM // 2, -1.0, 1.0)
    two = lambda t: np.concatenate([t, t], axis=-1).astype(f32)

    def table(fn, sgn):
        tr = jnp.broadcast_to(jnp.asarray(two(fn(ang_r)) * sgn, F32)[:, None, :], (rows, GRID_W, ROPE_AXIS_DIM))
        tc = jnp.broadcast_to(jnp.asarray(two(fn(ang_c)) * sgn, F32)[None, :, :], (rows, GRID_W, ROPE_AXIS_DIM))
        return jnp.concatenate([tr, tc, tr, tc], axis=-1).reshape(n_tokens, LANES)

    cos = jnp.concatenate([jnp.ones((n_ctx, LANES), F32), table(np.cos, 1.0)], axis=0)
    sin = jnp.concatenate([jnp.zeros((n_ctx, LANES), F32), table(np.sin, sign)], axis=0)
    return cos, sin


def _block_tri(n, upper):
    i = np.arange(n)
    same = (i[:, None] // GLA_CHUNK) == (i[None, :] // GLA_CHUNK)
    tri = (i[None, :] >= i[:, None]) if upper else (i[:, None] >= i[None, :])
    return jnp.asarray((same & tri).astype(np.float32)).astype(BF16)


def _attn_tiles(l_, lt):
    tq = 1024 if l_ % 1024 == 0 else l_
    nsub = 2 if l_ % (2 * tq) == 0 else 1
    for tk in (768, 512, 384, 256, 128):
        if lt % tk == 0:
            return tq, tk, nsub
    raise ValueError("key length must be a multiple of 128")


def kernel(x, c, ctx, c_ctx, w_ada, b_ada, w_in, gla_w_decay, gla_b_decay, gla_norm, diff_q_norm,
           diff_k_norm, diff_lambda, diff_norm, w_br_gla, w_br_diff, w_out):
    b_, l_, d = x.shape
    n_ctx = ctx.shape[1]
    depth = w_ada.shape[0]
    assert depth == 1, "single-layer block"
    layer = 0
    lam_init = 0.8 - 0.6 * math.exp(-0.3 * layer)
    half_k = d // 2

    n_mod = -(-(b_ + 1) // SUBLANES) * SUBLANES
    cc = jnp.zeros((n_mod, d), F32).at[:b_].set(c).at[b_].set(c_ctx)
    mod, lam_tile = _mod_call(cc, w_ada[layer], b_ada[layer][None, :], diff_lambda[layer], lam_init)
    mod3 = mod.reshape(n_mod, 1, 3 * d)
    lam = lam_tile[0, :1]

    w = w_in[layer]
    sizes = (half_k, half_k, d, d, 2 * GLA_RANK, d, d, d, d, d, d)
    names = ("gq", "gk", "gv", "gg", "lr", "dq", "dk", "dv", "dg", "mg", "md")
    starts = np.concatenate([[0], np.cumsum(sizes)])
    lr_at = names.index("lr")
    lr_lo, lr_hi = int(starts[lr_at]), int(starts[lr_at + 1])
    col_scale = np.ones((1, int(starts[-1])), np.float32)
    for name in ("gg", "dg", "mg", "md"):
        i = names.index(name)
        col_scale[:, int(starts[i]):int(starts[i + 1])] = 0.5
    col_scale = jnp.asarray(col_scale, BF16)
    w16 = w.astype(BF16)
    w_a = w16[:, :lr_lo] * col_scale[:, :lr_lo]
    w_b = w16[:, lr_hi:] * col_scale[:, lr_hi:]
    w_lr = jnp.concatenate([w[:, lr_lo:lr_hi]] * 3 + [jnp.zeros((d, LANES - 6 * GLA_RANK), w.dtype)],
                           axis=1).astype(BF16)
    offs = {"lr": (2, 0, LANES)}
    for i, name in enumerate(names):
        if i < lr_at:
            offs[name] = (0, int(starts[i]), int(starts[i + 1]))
        elif i > lr_at:
            offs[name] = (1, int(starts[i]) - lr_hi, int(starts[i + 1]) - lr_hi)

    zk = jnp.zeros((GLA_RANK, half_k), F32)
    wd_f32 = jnp.concatenate([jnp.concatenate([gla_w_decay[layer, 0], zk], axis=1),
                              jnp.concatenate([zk, gla_w_decay[layer, 1]], axis=1)], axis=0)
    bd_f32 = jnp.concatenate([gla_b_decay[layer, 0], gla_b_decay[layer, 1]])[None, :]
    hi_lo = lambda t: (t.astype(BF16), (t - t.astype(BF16).astype(F32)).astype(BF16))
    (wd_hi, wd_lo), (bd_hi, bd_lo) = hi_lo(wd_f32), hi_lo(bd_f32)
    wdec = jnp.concatenate([wd_hi, wd_lo, wd_hi, bd_hi, bd_lo,
                            jnp.zeros((LANES - 6 * GLA_RANK - 2, 2 * half_k), BF16)], axis=0)

    cos, sin_signed = _rope_tables(l_, n_ctx)
    lane = np.arange(LANES)
    partner = np.where(lane % ROPE_AXIS_DIM < ROPE_AXIS_DIM // 2,
                       lane + ROPE_AXIS_DIM // 2, lane - ROPE_AXIS_DIM // 2)
    g_q = jnp.tile(diff_q_norm[layer], LANES // DIFF_DK) * (DIFF_DK ** -0.5 * LOG2E)
    g_k = jnp.tile(diff_k_norm[layer], LANES // DIFF_DK)
    gain_rows = [g_q, g_q[partner], g_k, g_k[partner]]
    gains = jnp.stack(gain_rows + [jnp.zeros((LANES,), F32)] * (SUBLANES - len(gain_rows)))

    trif = _block_tri(ROW_TILE, upper=False) * (1.0 / GLA_TAU)
    trib = _block_tri(ROW_TILE, upper=True) * (1.0 / GLA_TAU)
    gi = np.arange(MXU_TILE)
    gsum = jnp.asarray((gi[:, None] // DIFF_DK == gi[None, :] // DIFF_DK).astype(np.float32)
                       / DIFF_DK).astype(BF16)

    (gq, gk, gvt, cumf, cumb, gg, dg, mg, md, dq, dk, dvt) = _proj_call(
        x, ctx, mod3, w_a, w_b, w_lr, offs, wdec, gains, cos, sin_signed, trif, trib, gsum)

    o_f, o_b = _gla_call(gq, gk, gvt, cumf, cumb)

    tq, tk, nsub = _attn_tiles(l_, l_ + n_ctx)
    score_bound = DIFF_DK ** 0.5 * jnp.max(jnp.abs(diff_q_norm[layer])) * jnp.max(jnp.abs(diff_k_norm[layer]))
    bounded = (score_bound * BOUND_MARGIN <= SCORE_BOUND).astype(F32)
    scal = jnp.concatenate([lam, bounded[None]])
    o_d = _attn_call(scal, dq, dk, dvt, tq, tk, nsub)

    ggain = jnp.tile(gla_norm[layer], GLA_HEADS)[None, :]
    dgain = (jnp.tile(diff_norm[layer], d // diff_norm.shape[1]) * (1.0 - lam_init))[None, :]
    tm = 512 if l_ % 512 == 0 else l_
    return _combine_call(x, mod3, o_f, o_b, o_d, gg, dg, mg, md, ggain, dgain,
                         w_br_gla[layer].astype(BF16), w_br_diff[layer].astype(BF16),
                         w_out[layer].astype(BF16), tm)
```

```python
import functools
import math

import jax
import jax.numpy as jnp
import numpy as np
from jax import lax
from jax.experimental import pallas as pl
from jax.experimental.pallas import tpu as pltpu

F32 = jnp.float32
BF16 = jnp.bfloat16
HIGHEST = lax.Precision.HIGHEST

EPS = 1e-6
GRID_W = 64
GLA_HEADS = 4
GLA_RANK = 16
GLA_TAU = 16.0
GLA_CHUNK = 64
DIFF_DK = 64
ROPE_BASE = 10000.0
ROPE_AXIS_DIM = DIFF_DK // 2
LANES = 128
SUBLANES = 8
MXU_TILE = 256
ROW_TILE = 256
VMEM_LIMIT = 56 * 1024 * 1024
LOG2E = math.log2(math.e)
SCORE_BOUND = 40.0
BOUND_MARGIN = 1.02


def _dot(a, b):
    return jnp.dot(a, b, preferred_element_type=F32)


def _dot_nt(a, b):
    return lax.dot_general(a, b, (((1,), (1,)), ((), ())), preferred_element_type=F32)


def _dot_exact(a, b):
    return lax.dot_general(a, b, (((1,), (0,)), ((), ())), precision=HIGHEST,
                           preferred_element_type=F32)


def _split_bf16(a):
    hi = a.astype(BF16)
    return hi, (a - hi.astype(F32)).astype(BF16)


def _sigmoid(t):
    return 0.5 * jnp.tanh(0.5 * t) + 0.5


def _silu(t):
    return t * _sigmoid(t)


def _mod_kernel(lam_init, cc_ref, w_ref, b_ref, lamp_ref, mod_ref, lam_ref):
    cc = cc_ref[...]
    mod_ref[...] = _dot_exact(_silu(cc), w_ref[...]) + b_ref[...]
    lp = lamp_ref[...]
    e1 = jnp.exp(jnp.sum(lp[0:1] * lp[1:2], axis=-1, keepdims=True))
    e2 = jnp.exp(jnp.sum(lp[2:3] * lp[3:4], axis=-1, keepdims=True))
    lam_ref[...] = jnp.broadcast_to(e1 - e2 + lam_init, lam_ref.shape)


def _mod_call(cc, w_ada, b_ada, lam_params, lam_init):
    rows, d = cc.shape
    n = w_ada.shape[1]
    return pl.pallas_call(
        functools.partial(_mod_kernel, lam_init),
        grid=(n // d,),
        in_specs=[
            pl.BlockSpec((rows, d), lambda j: (0, 0)),
            pl.BlockSpec((d, d), lambda j: (0, j)),
            pl.BlockSpec((1, d), lambda j: (0, j)),
            pl.BlockSpec(lam_params.shape, lambda j: (0, 0)),
        ],
        out_specs=[
            pl.BlockSpec((rows, d), lambda j: (0, j)),
            pl.BlockSpec((rows, LANES), lambda j: (0, 0)),
        ],
        out_shape=[jax.ShapeDtypeStruct((rows, n), F32), jax.ShapeDtypeStruct((rows, LANES), F32)],
        compiler_params=pltpu.CompilerParams(dimension_semantics=("arbitrary",)),
        name="mod",
    )(cc, w_ada, b_ada, lam_params)


def _group_norm_rope(p, gsum_ref, cos_g, sin_g, lo_mask):
    tm, d = p.shape
    sq = (p * p).astype(BF16)
    slabs = []
    for s in range(d // MXU_TILE):
        slabs.append(_dot(sq[:, s * MXU_TILE:(s + 1) * MXU_TILE], gsum_ref[...]))
    n = p * lax.rsqrt(jnp.concatenate(slabs, axis=1) + EPS)
    outs = []
    for h in range(d // LANES):
        sl = n[:, h * LANES:(h + 1) * LANES]
        up = pltpu.roll(sl, LANES - ROPE_AXIS_DIM // 2, axis=1)
        dn = pltpu.roll(sl, ROPE_AXIS_DIM // 2, axis=1)
        outs.append(sl * cos_g + jnp.where(lo_mask, up, dn) * sin_g)
    return jnp.concatenate(outs, axis=1)


def _proj_kernel(offs, x_ref, ctx_ref, shift_ref, scale_ref, wa_ref, wb_ref, wlr_ref, wdec_ref,
                 gains_ref, cos_ref, sin_ref, trif_ref, trib_ref, gsum_ref,
                 gq_ref, gk_ref, gvt_ref, cf_ref, cb_ref, gg_ref, dg_ref, mg_ref, md_ref,
                 dq_ref, dk_ref, dvt_ref):
    t = pl.program_id(1)
    latent = t > 0
    xin = jnp.where(latent, x_ref[0], ctx_ref[0])
    ms = jnp.mean(xin * xin, axis=-1, keepdims=True)
    h = xin * lax.rsqrt(ms + EPS) * (1.0 + scale_ref[0]) + shift_ref[0]
    hb = h.astype(BF16)

    def proj(name):
        which, a, b = offs[name]
        return _dot(hb, (wa_ref, wb_ref, wlr_ref)[which][:, a:b])

    lr3 = proj("lr")
    lr_hi, lr_lo = _split_bf16(lr3)
    lane = lax.broadcasted_iota(jnp.int32, lr3.shape, 1)
    lhs = jnp.where(lane < 4 * GLA_RANK, lr_hi,
                    jnp.where(lane < 6 * GLA_RANK, lr_lo, jnp.ones_like(lr_lo)))
    z = _dot(lhs, wdec_ref[...])
    la = jnp.minimum(z, 0.0) - jnp.log(1.0 + jnp.exp(-jnp.abs(z)))
    half = la.shape[1] // 2
    la_hi, la_lo = _split_bf16(la)

    gq_ref[0] = (proj("gq") * (gq_ref.shape[2] // GLA_HEADS) ** -0.5).astype(BF16)
    gk_ref[0] = proj("gk").astype(BF16)
    gvt_ref[0] = proj("gv").T.astype(BF16)

    for name, ref in (("gg", gg_ref), ("dg", dg_ref)):
        hg = proj(name)
        ref[0] = (hg * jnp.tanh(hg) + hg).astype(BF16)
    for name, ref in (("mg", mg_ref), ("md", md_ref)):
        ref[0] = (0.5 * jnp.tanh(proj(name)) + 0.5).astype(BF16)

    cf_ref[0] = _dot(trif_ref[...], la_hi[:, :half]) + _dot(trif_ref[...], la_lo[:, :half])
    cb_ref[0] = _dot(trib_ref[...], la_hi[:, half:]) + _dot(trib_ref[...], la_lo[:, half:])

    cos = cos_ref[...]
    sin_signed = sin_ref[...]
    lane = lax.broadcasted_iota(jnp.int32, cos.shape, 1)
    lo_mask = (lane % ROPE_AXIS_DIM) < (ROPE_AXIS_DIM // 2)
    dq_ref[0] = _group_norm_rope(proj("dq"), gsum_ref, cos * gains_ref[0:1], sin_signed * gains_ref[1:2],
                                 lo_mask).astype(BF16)
    dk_ref[0] = _group_norm_rope(proj("dk"), gsum_ref, cos * gains_ref[2:3], sin_signed * gains_ref[3:4],
                                 lo_mask).astype(BF16)
    dvt_ref[0] = proj("dv").T.astype(BF16)


def _proj_call(x, ctx, mod3, w_a, w_b, w_lr, offs, wdec, gains, cos, sin_signed,
               trif, trib, gsum):
    b_, l_, d = x.shape
    tm = ROW_TILE
    assert ctx.shape[1] == tm and l_ % tm == 0
    nt = l_ // tm + 1
    lt = l_ + tm
    n_ctx_row = b_
    half_k = offs["gq"][2] - offs["gq"][1]

    lat = lambda b, t: (b, jnp.maximum(t - 1, 0), 0)
    allr = lambda b, t: (b, t, 0)
    const2 = lambda b, t: (0, 0)
    once = pl.Buffered(1)

    def mod_spec(col):
        return pl.BlockSpec((1, 1, d), lambda b, t: (jnp.where(t == 0, n_ctx_row, b), 0, col))

    in_specs = [
        pl.BlockSpec((1, tm, d), lat),
        pl.BlockSpec((1, tm, d), lambda b, t: (b, 0, 0)),
        mod_spec(0), mod_spec(1),
        pl.BlockSpec(w_a.shape, const2, pipeline_mode=once),
        pl.BlockSpec(w_b.shape, const2, pipeline_mode=once),
        pl.BlockSpec(w_lr.shape, const2, pipeline_mode=once),
        pl.BlockSpec(wdec.shape, const2, pipeline_mode=once),
        pl.BlockSpec(gains.shape, const2, pipeline_mode=once),
        pl.BlockSpec((tm, LANES), lambda b, t: (t, 0)),
        pl.BlockSpec((tm, LANES), lambda b, t: (t, 0)),
        pl.BlockSpec(trif.shape, const2, pipeline_mode=once),
        pl.BlockSpec(trib.shape, const2, pipeline_mode=once),
        pl.BlockSpec(gsum.shape, const2, pipeline_mode=once),
    ]
    out_specs = [
        pl.BlockSpec((1, tm, half_k), lat),
        pl.BlockSpec((1, tm, half_k), allr),
        pl.BlockSpec((1, d, tm), lambda b, t: (b, 0, t)),
        pl.BlockSpec((1, tm, half_k), allr),
        pl.BlockSpec((1, tm, half_k), allr),
        pl.BlockSpec((1, tm, d), lat),
        pl.BlockSpec((1, tm, d), lat),
        pl.BlockSpec((1, tm, d), lat),
        pl.BlockSpec((1, tm, d), lat),
        pl.BlockSpec((1, tm, d), lat),
        pl.BlockSpec((1, tm, d), allr),
        pl.BlockSpec((1, d, tm), lambda b, t: (b, 0, t)),
    ]
    sd = jax.ShapeDtypeStruct
    out_shape = [
        sd((b_, l_, half_k), BF16), sd((b_, lt, half_k), BF16), sd((b_, d, lt), BF16),
        sd((b_, lt, half_k), F32), sd((b_, lt, half_k), F32),
        sd((b_, l_, d), BF16), sd((b_, l_, d), BF16), sd((b_, l_, d), BF16), sd((b_, l_, d), BF16),
        sd((b_, l_, d), BF16), sd((b_, lt, d), BF16), sd((b_, d, lt), BF16),
    ]
    return pl.pallas_call(
        functools.partial(_proj_kernel, offs),
        grid=(b_, nt),
        in_specs=in_specs, out_specs=out_specs, out_shape=out_shape,
        compiler_params=pltpu.CompilerParams(dimension_semantics=("arbitrary", "arbitrary"),
                                             vmem_limit_bytes=VMEM_LIMIT),
        name="proj",
    )(x, ctx, mod3, mod3, w_a, w_b, w_lr, wdec, gains, cos, sin_signed, trif, trib, gsum)


def _chunk_rows(rows, dk):
    return jnp.concatenate([jnp.broadcast_to(r, (GLA_CHUNK, dk)) for r in rows], axis=0)


def _gla_direction(forward, q_ref, k_ref, vt_ref, c_ref, o_ref, st_ref):
    tm = k_ref.shape[1]
    nh = GLA_HEADS
    dk = k_ref.shape[2] // nh
    dv = vt_ref.shape[1] // nh
    nc = tm // GLA_CHUNK
    half = GLA_CHUNK // 2
    ri = lax.broadcasted_iota(jnp.int32, (tm, tm), 0)
    ci = lax.broadcasted_iota(jnp.int32, (tm, tm), 1)
    same = (ri // GLA_CHUNK) == (ci // GLA_CHUNK)
    mask = same & ((ri >= ci) if forward else (ci >= ri))
    row_chunk = lax.broadcasted_iota(jnp.int32, (tm, dk), 0) // GLA_CHUNK
    order = list(range(nc)) if forward else list(reversed(range(nc)))
    zrow = jnp.zeros((1, dk), F32)
    for h in range(nh):
        ks = slice(h * dk, (h + 1) * dk)
        vs = slice(h * dv, (h + 1) * dv)
        cum = c_ref[0, :, ks]
        q = q_ref[0, :, ks].astype(F32)
        k = k_ref[0, :, ks].astype(F32)
        vt = vt_ref[0, vs, :]
        last, ref = [], []
        for c in range(nc):
            base = c * GLA_CHUNK
            li = base + GLA_CHUNK - 1 if forward else base
            ce = base + half - 1 if forward else base + half
            last.append(cum[li:li + 1])
            ref.append(cum[ce:ce + 1])
        lastb = _chunk_rows(last, dk)
        refb = _chunk_rows(ref, dk)
        qe = (q * jnp.exp(cum - refb)).astype(BF16)
        ke = (k * jnp.exp(refb - cum)).astype(BF16)
        a = jnp.where(mask, _dot_nt(qe, ke), 0.0).astype(BF16)
        o_intra = _dot_nt(a, vt)
        p_start, p_end, run = {}, {}, zrow
        for c in order:
            p_start[c] = run
            run = run + last[c]
            p_end[c] = run
        p_total = run
        qd = q * jnp.exp(cum)
        kd = k * jnp.exp(lastb - cum)
        sources = order[:-1]
        k_cols = [jnp.where(row_chunk == s, kd, 0.0).astype(BF16) for s in sources]
        k_cols.append((kd * jnp.exp(_chunk_rows([p_total - p_end[c] for c in range(nc)], dk))).astype(BF16))
        ut = _dot(vt, jnp.concatenate(k_cols, axis=1))
        st = st_ref[h]
        wt = jnp.concatenate([st, ut[:, :(nc - 1) * dk]], axis=1).astype(BF16)
        q_cols = [(qd * jnp.exp(_chunk_rows([p_start[c] for c in range(nc)], dk))).astype(BF16)]
        for i, s in enumerate(sources):
            later = order[i + 1:]
            fac = [jnp.exp(p_start[c] - p_end[s]) if c in later else zrow for c in range(nc)]
            q_cols.append((qd * _chunk_rows(fac, dk)).astype(BF16))
        o_inter = _dot_nt(jnp.concatenate(q_cols, axis=1), wt)
        st_ref[h] = st * jnp.exp(p_total) + ut[:, (nc - 1) * dk:]
        o_ref[0, :, vs] = (o_intra + o_inter).astype(o_ref.dtype)


def _gla_kernel(qf_ref, kf_ref, vtf_ref, cf_ref, qb_ref, kb_ref, vtb_ref, cb_ref,
                of_ref, ob_ref, sf_ref, sb_ref):
    @pl.when(pl.program_id(1) == 0)
    def _():
        sf_ref[...] = jnp.zeros_like(sf_ref)
        sb_ref[...] = jnp.zeros_like(sb_ref)

    _gla_direction(True, qf_ref, kf_ref, vtf_ref, cf_ref, of_ref, sf_ref)
    _gla_direction(False, qb_ref, kb_ref, vtb_ref, cb_ref, ob_ref, sb_ref)


def _gla_call(gq, gk, gvt, cumf, cumb):
    b_, l_, kw = gq.shape
    lt = gk.shape[1]
    d = gvt.shape[1]
    tm = ROW_TILE
    nt = lt // tm
    nlat = l_ // tm
    dk = kw // GLA_HEADS
    dv = d // GLA_HEADS

    bwd_blk = lambda s: jnp.where(s == 0, 0, nt - s)
    f_lat = lambda b, s: (b, jnp.maximum(s - 1, 0), 0)
    f_all = lambda b, s: (b, s, 0)
    b_lat = lambda b, s: (b, jnp.where(s == 0, nlat - 1, nlat - s), 0)
    b_all = lambda b, s: (b, bwd_blk(s), 0)
    return pl.pallas_call(
        _gla_kernel,
        grid=(b_, nt),
        in_specs=[
            pl.BlockSpec((1, tm, kw), f_lat), pl.BlockSpec((1, tm, kw), f_all),
            pl.BlockSpec((1, d, tm), lambda b, s: (b, 0, s)),
            pl.BlockSpec((1, tm, kw), f_all),
            pl.BlockSpec((1, tm, kw), b_lat), pl.BlockSpec((1, tm, kw), b_all),
            pl.BlockSpec((1, d, tm), lambda b, s: (b, 0, bwd_blk(s))),
            pl.BlockSpec((1, tm, kw), b_all),
        ],
        out_specs=[pl.BlockSpec((1, tm, d), f_lat), pl.BlockSpec((1, tm, d), b_lat)],
        out_shape=[jax.ShapeDtypeStruct((b_, l_, d), BF16), jax.ShapeDtypeStruct((b_, l_, d), BF16)],
        scratch_shapes=[pltpu.VMEM((GLA_HEADS, dv, dk), F32), pltpu.VMEM((GLA_HEADS, dv, dk), F32)],
        compiler_params=pltpu.CompilerParams(dimension_semantics=("arbitrary", "arbitrary"),
                                             vmem_limit_bytes=VMEM_LIMIT),
        name="gla",
    )(gq, gk, gvt, cumf, gq, gk, gvt, cumb)


def _attn_kernel(tk, scal_ref, q_ref, k_ref, vt_ref, o_ref, acc1, acc2, m1, l1, m2, l2):
    nsub = acc1.shape[0]
    tq = q_ref.shape[1] // nsub
    nk = k_ref.shape[1] // tk
    lam = scal_ref[0]

    def streams(sub):
        q = q_ref[0, sub * tq:(sub + 1) * tq, :]
        lane = lax.broadcasted_iota(jnp.int32, q.shape, 1)
        zero = jnp.zeros_like(q)
        q1 = jnp.where(lane < DIFF_DK, q, zero)
        q2 = jnp.where(lane >= DIFF_DK, q, zero)
        return ((q1, acc1.at[sub], m1.at[sub], l1.at[sub]), (q2, acc2.at[sub], m2.at[sub], l2.at[sub]))

    def finish(sub):
        inv1 = 1.0 / jnp.sum(l1[sub], axis=0, keepdims=True)
        inv2 = 1.0 / jnp.sum(l2[sub], axis=0, keepdims=True)
        o = acc1[sub] * inv1 - lam * (acc2[sub] * inv2)
        o = o * lax.rsqrt(jnp.mean(o * o, axis=0, keepdims=True) + EPS)
        o_ref[0, :, sub * tq:(sub + 1) * tq] = o.astype(o_ref.dtype)

    def bounded_tiles(sub):
        st = streams(sub)

        def scores(j):
            kk = k_ref[0, j * tk:(j + 1) * tk, :]
            return [_dot_nt(kk, qq) for qq, _, _, _ in st]

        s_next = scores(0)
        for j in range(nk):
            s_cur = s_next
            if j + 1 < nk:
                s_next = scores(j + 1)
            vt = vt_ref[0, :, j * tk:(j + 1) * tk]
            for s, (_, acc, _, l) in zip(s_cur, st):
                p = jnp.exp2(s)
                l_part = jnp.sum(p.reshape(tk // SUBLANES, SUBLANES, tq), axis=0)
                pv = _dot(vt, p.astype(BF16))
                l[...] = l_part if j == 0 else l[...] + l_part
                acc[...] = pv if j == 0 else acc[...] + pv

    def online_tiles(sub):
        st = streams(sub)
        for _, acc, m, l in st:
            acc[...] = jnp.zeros_like(acc)
            l[...] = jnp.zeros_like(l)
            m[...] = jnp.full_like(m, -1e30)

        def body(j, carry):
            off = pl.multiple_of(j * tk, tk)
            kk = k_ref[0, pl.ds(off, tk), :]
            vt = vt_ref[0, :, pl.ds(off, tk)]
            for qq, acc, m, l in st:
                s = _dot_nt(kk, qq)
                m_new = jnp.maximum(m[...], jnp.max(s, axis=0, keepdims=True))
                alpha = jnp.exp2(m[...] - m_new)
                p = jnp.exp2(s - m_new)
                l[0:1] = alpha * l[0:1] + jnp.sum(p, axis=0, keepdims=True)
                acc[...] = alpha * acc[...] + _dot(vt, p.astype(BF16))
                m[...] = m_new
            return carry

        lax.fori_loop(0, nk, body, 0)

    bounded = scal_ref[1] != 0.0

    @pl.when(bounded)
    def _():
        for sub in range(nsub):
            bounded_tiles(sub)
            finish(sub)

    @pl.when(jnp.logical_not(bounded))
    def _():
        for sub in range(nsub):
            online_tiles(sub)
            finish(sub)


def _attn_call(scal, dq, dk, dvt, tq, tk, nsub):
    b_, l_, d = dq.shape
    lt = dk.shape[1]
    nh = d // LANES
    tstep = tq * nsub
    assert l_ % tstep == 0 and lt % tk == 0
    return pl.pallas_call(
        functools.partial(_attn_kernel, tk),
        grid=(b_, nh, l_ // tstep),
        in_specs=[
            pl.BlockSpec(memory_space=pltpu.SMEM),
            pl.BlockSpec((1, tstep, LANES), lambda b, h, i: (b, i, h)),
            pl.BlockSpec((1, lt, LANES), lambda b, h, i: (b, 0, h)),
            pl.BlockSpec((1, LANES, lt), lambda b, h, i: (b, h, 0)),
        ],
        out_specs=pl.BlockSpec((1, LANES, tstep), lambda b, h, i: (b, h, i)),
        out_shape=jax.ShapeDtypeStruct((b_, d, l_), BF16),
        scratch_shapes=[pltpu.VMEM((nsub, LANES, tq), F32), pltpu.VMEM((nsub, LANES, tq), F32),
                        pltpu.VMEM((nsub, 1, tq), F32), pltpu.VMEM((nsub, SUBLANES, tq), F32),
                        pltpu.VMEM((nsub, 1, tq), F32), pltpu.VMEM((nsub, SUBLANES, tq), F32)],
        compiler_params=pltpu.CompilerParams(
            dimension_semantics=("arbitrary", "arbitrary", "arbitrary"),
            vmem_limit_bytes=VMEM_LIMIT),
        name="diffattn",
    )(scal, dq, dk, dvt)


def _combine_kernel(x_ref, g_ref, of_ref, ob_ref, od_ref, gg_ref, dg_ref, mg_ref, md_ref,
                    ggain_ref, dgain_ref, wg_ref, wd_ref, wo_ref, out_ref):
    d = x_ref.shape[2]
    dv = d // GLA_HEADS
    og = of_ref[0].astype(F32) + ob_ref[0].astype(F32)
    normed = []
    for h in range(GLA_HEADS):
        sl = og[:, h * dv:(h + 1) * dv]
        normed.append(sl * lax.rsqrt(jnp.mean(sl * sl, axis=-1, keepdims=True) + EPS))
    a = jnp.concatenate(normed, axis=1) * ggain_ref[...] * gg_ref[0].astype(F32)
    bb = od_ref[0].astype(F32).T * dgain_ref[...] * dg_ref[0].astype(F32)
    y = (mg_ref[0].astype(F32) * _dot(a.astype(BF16), wg_ref[...])
         + md_ref[0].astype(F32) * _dot(bb.astype(BF16), wd_ref[...]))
    out_ref[0] = x_ref[0] + g_ref[0] * _dot(y.astype(BF16), wo_ref[...])


def _combine_call(x, mod3, o_f, o_b, o_d, gg, dg, mg, md, ggain, dgain, wg, wd, wo, tm):
    b_, l_, d = x.shape
    row = lambda b, i: (b, i, 0)
    const2 = lambda b, i: (0, 0)
    once = pl.Buffered(1)
    big = pl.BlockSpec((1, tm, d), row)
    wspec = pl.BlockSpec((d, d), const2, pipeline_mode=once)
    return pl.pallas_call(
        _combine_kernel,
        grid=(b_, l_ // tm),
        in_specs=[big, pl.BlockSpec((1, 1, d), lambda b, i: (b, 0, 2)),
                  big, big, pl.BlockSpec((1, d, tm), lambda b, i: (b, 0, i)), big, big, big, big,
                  pl.BlockSpec((1, d), const2, pipeline_mode=once),
                  pl.BlockSpec((1, d), const2, pipeline_mode=once),
                  wspec, wspec, wspec],
        out_specs=big,
        out_shape=jax.ShapeDtypeStruct(x.shape, x.dtype),
        compiler_params=pltpu.CompilerParams(dimension_semantics=("arbitrary", "arbitrary"),
                                             vmem_limit_bytes=VMEM_LIMIT),
        name="combine",
    )(x, mod3, o_f, o_b, o_d, gg, dg, mg, md, ggain, dgain, wg, wd, wo)


def _rope_tables(n_tokens, n_ctx):
    f32 = np.float32
    rows = n_tokens // GRID_W
    inv_freq = f32(ROPE_BASE) ** (-np.arange(0, ROPE_AXIS_DIM, 2, dtype=f32) / f32(ROPE_AXIS_DIM))
    ang_r = (np.arange(rows, dtype=f32)[:, None] * inv_freq).astype(f32)
    ang_c = (np.arange(GRID_W, dtype=f32)[:, None] * inv_freq).astype(f32)
    sign = np.where(np.arange(ROPE_AXIS_DIM) < ROPE_AXIS_DIM // 2, -1.0, 1.0)
    two = lambda t: np.concatenate([t, t], axis=-1).astype(f32)

    def table(fn, sgn):
        tr = jnp.broadcast_to(jnp.asarray(two(fn(ang_r)) * sgn, F32)[:, None, :], (rows, GRID_W, ROPE_AXIS_DIM))
        tc = jnp.broadcast_to(jnp.asarray(two(fn(ang_c)) * sgn, F32)[None, :, :], (rows, GRID_W, ROPE_AXIS_DIM))
        return jnp.concatenate([tr, tc, tr, tc], axis=-1).reshape(n_tokens, LANES)

    cos = jnp.concatenate([jnp.ones((n_ctx, LANES), F32), table(np.cos, 1.0)], axis=0)
    sin = jnp.concatenate([jnp.zeros((n_ctx, LANES), F32), table(np.sin, sign)], axis=0)
    return cos, sin


def _block_tri(n, upper):
    i = np.arange(n)
    same = (i[:, None] // GLA_CHUNK) == (i[None, :] // GLA_CHUNK)
    tri = (i[None, :] >= i[:, None]) if upper else (i[:, None] >= i[None, :])
    return jnp.asarray((same & tri).astype(np.float32)).astype(BF16)


def _attn_tiles(l_, lt):
    tq = 1024 if l_ % 1024 == 0 else l_
    nsub = 2 if l_ % (2 * tq) == 0 else 1
    for tk in (768, 512, 384, 256, 128):
        if lt % tk == 0:
            return tq, tk, nsub
    raise ValueError("key length must be a multiple of 128")


def kernel(x, c, ctx, c_ctx, w_ada, b_ada, w_in, gla_w_decay, gla_b_decay, gla_norm, diff_q_norm,
           diff_k_norm, diff_lambda, diff_norm, w_br_gla, w_br_diff, w_out):
    b_, l_, d = x.shape
    n_ctx = ctx.shape[1]
    depth = w_ada.shape[0]
    assert depth == 1, "single-layer block"
    layer = 0
    lam_init = 0.8 - 0.6 * math.exp(-0.3 * layer)
    half_k = d // 2

    n_mod = -(-(b_ + 1) // SUBLANES) * SUBLANES
    cc = jnp.zeros((n_mod, d), F32).at[:b_].set(c).at[b_].set(c_ctx)
    mod, lam_tile = _mod_call(cc, w_ada[layer], b_ada[layer][None, :], diff_lambda[layer], lam_init)
    mod3 = mod.reshape(n_mod, 1, 3 * d)
    lam = lam_tile[0, :1]

    w = w_in[layer]
    sizes = (half_k, half_k, d, d, 2 * GLA_RANK, d, d, d, d, d, d)
    names = ("gq", "gk", "gv", "gg", "lr", "dq", "dk", "dv", "dg", "mg", "md")
    starts = np.concatenate([[0], np.cumsum(sizes)])
    lr_at = names.index("lr")
    lr_lo, lr_hi = int(starts[lr_at]), int(starts[lr_at + 1])
    col_scale = np.ones((1, int(starts[-1])), np.float32)
    for name in ("gg", "dg", "mg", "md"):
        i = names.index(name)
        col_scale[:, int(starts[i]):int(starts[i + 1])] = 0.5
    col_scale = jnp.asarray(col_scale, BF16)
    w16 = w.astype(BF16)
    w_a = w16[:, :lr_lo] * col_scale[:, :lr_lo]
    w_b = w16[:, lr_hi:] * col_scale[:, lr_hi:]
    w_lr = jnp.concatenate([w[:, lr_lo:lr_hi]] * 3 + [jnp.zeros((d, LANES - 6 * GLA_RANK), w.dtype)],
                           axis=1).astype(BF16)
    offs = {"lr": (2, 0, LANES)}
    for i, name in enumerate(names):
        if i < lr_at:
            offs[name] = (0, int(starts[i]), int(starts[i + 1]))
        elif i > lr_at:
            offs[name] = (1, int(starts[i]) - lr_hi, int(starts[i + 1]) - lr_hi)

    zk = jnp.zeros((GLA_RANK, half_k), F32)
    wd_f32 = jnp.concatenate([jnp.concatenate([gla_w_decay[layer, 0], zk], axis=1),
                              jnp.concatenate([zk, gla_w_decay[layer, 1]], axis=1)], axis=0)
    bd_f32 = jnp.concatenate([gla_b_decay[layer, 0], gla_b_decay[layer, 1]])[None, :]
    hi_lo = lambda t: (t.astype(BF16), (t - t.astype(BF16).astype(F32)).astype(BF16))
    (wd_hi, wd_lo), (bd_hi, bd_lo) = hi_lo(wd_f32), hi_lo(bd_f32)
    wdec = jnp.concatenate([wd_hi, wd_lo, wd_hi, bd_hi, bd_lo,
                            jnp.zeros((LANES - 6 * GLA_RANK - 2, 2 * half_k), BF16)], axis=0)

    cos, sin_signed = _rope_tables(l_, n_ctx)
    lane = np.arange(LANES)
    partner = np.where(lane % ROPE_AXIS_DIM < ROPE_AXIS_DIM // 2,
                       lane + ROPE_AXIS_DIM // 2, lane - ROPE_AXIS_DIM // 2)
    g_q = jnp.tile(diff_q_norm[layer], LANES // DIFF_DK) * (DIFF_DK ** -0.5 * LOG2E)
    g_k = jnp.tile(diff_k_norm[layer], LANES // DIFF_DK)
    gain_rows = [g_q, g_q[partner], g_k, g_k[partner]]
    gains = jnp.stack(gain_rows + [jnp.zeros((LANES,), F32)] * (SUBLANES - len(gain_rows)))

    trif = _block_tri(ROW_TILE, upper=False) * (1.0 / GLA_TAU)
    trib = _block_tri(ROW_TILE, upper=True) * (1.0 / GLA_TAU)
    gi = np.arange(MXU_TILE)
    gsum = jnp.asarray((gi[:, None] // DIFF_DK == gi[None, :] // DIFF_DK).astype(np.float32)
                       / DIFF_DK).astype(BF16)

    (gq, gk, gvt, cumf, cumb, gg, dg, mg, md, dq, dk, dvt) = _proj_call(
        x, ctx, mod3, w_a, w_b, w_lr, offs, wdec, gains, cos, sin_signed, trif, trib, gsum)

    o_f, o_b = _gla_call(gq, gk, gvt, cumf, cumb)

    tq, tk, nsub = _attn_tiles(l_, l_ + n_ctx)
    score_bound = DIFF_DK ** 0.5 * jnp.max(jnp.abs(diff_q_norm[layer])) * jnp.max(jnp.abs(diff_k_norm[layer]))
    bounded = (score_bound * BOUND_MARGIN <= SCORE_BOUND).astype(F32)
    scal = jnp.concatenate([lam, bounded[None]])
    o_d = _attn_call(scal, dq, dk, dvt, tq, tk, nsub)

    ggain = jnp.tile(gla_norm[layer], GLA_HEADS)[None, :]
    dgain = (jnp.tile(diff_norm[layer], d // diff_norm.shape[1]) * (1.0 - lam_init))[None, :]
    tm = 512 if l_ % 512 == 0 else l_
    return _combine_call(x, mod3, o_f, o_b, o_d, gg, dg, mg, md, ggain, dgain,
                         w_br_gla[layer].astype(BF16), w_br_diff[layer].astype(BF16),
                         w_out[layer].astype(BF16), tm)
```

```python
import functools
import math

import jax
import jax.numpy as jnp
import numpy as np
from jax import lax
from jax.experimental import pallas as pl
from jax.experimental.pallas import tpu as pltpu

F32 = jnp.float32
BF16 = jnp.bfloat16
HIGHEST = lax.Precision.HIGHEST

EPS = 1e-6
GRID_W = 64
GLA_HEADS = 4
GLA_RANK = 16
GLA_TAU = 16.0
GLA_CHUNK = 64
DIFF_DK = 64
ROPE_BASE = 10000.0
ROPE_AXIS_DIM = DIFF_DK // 2
LANES = 128
SUBLANES = 8
MXU_TILE = 256
ROW_TILE = 256
VMEM_LIMIT = 56 * 1024 * 1024
LOG2E = math.log2(math.e)
SCORE_BOUND = 40.0
BOUND_MARGIN = 1.02


def _dot(a, b):
    return jnp.dot(a, b, preferred_element_type=F32)


def _dot_nt(a, b):
    return lax.dot_general(a, b, (((1,), (1,)), ((), ())), preferred_element_type=F32)


def _dot_exact(a, b):
    return lax.dot_general(a, b, (((1,), (0,)), ((), ())), precision=HIGHEST,
                           preferred_element_type=F32)


def _split_bf16(a):
    hi = a.astype(BF16)
    return hi, (a - hi.astype(F32)).astype(BF16)


def _sigmoid(t):
    return 0.5 * jnp.tanh(0.5 * t) + 0.5


def _silu(t):
    return t * _sigmoid(t)


def _mod_kernel(lam_init, cc_ref, w_ref, b_ref, lamp_ref, mod_ref, lam_ref):
    cc = cc_ref[...]
    mod_ref[...] = _dot_exact(_silu(cc), w_ref[...]) + b_ref[...]
    lp = lamp_ref[...]
    e1 = jnp.exp(jnp.sum(lp[0:1] * lp[1:2], axis=-1, keepdims=True))
    e2 = jnp.exp(jnp.sum(lp[2:3] * lp[3:4], axis=-1, keepdims=True))
    lam_ref[...] = jnp.broadcast_to(e1 - e2 + lam_init, lam_ref.shape)


def _mod_call(cc, w_ada, b_ada, lam_params, lam_init):
    rows, d = cc.shape
    n = w_ada.shape[1]
    return pl.pallas_call(
        functools.partial(_mod_kernel, lam_init),
        grid=(n // d,),
        in_specs=[
            pl.BlockSpec((rows, d), lambda j: (0, 0)),
            pl.BlockSpec((d, d), lambda j: (0, j)),
            pl.BlockSpec((1, d), lambda j: (0, j)),
            pl.BlockSpec(lam_params.shape, lambda j: (0, 0)),
        ],
        out_specs=[
            pl.BlockSpec((rows, d), lambda j: (0, j)),
            pl.BlockSpec((rows, LANES), lambda j: (0, 0)),
        ],
        out_shape=[jax.ShapeDtypeStruct((rows, n), F32), jax.ShapeDtypeStruct((rows, LANES), F32)],
        compiler_params=pltpu.CompilerParams(dimension_semantics=("arbitrary",)),
        name="mod",
    )(cc, w_ada, b_ada, lam_params)


def _group_norm_rope(p, gsum_ref, cos_g, sin_g, lo_mask):
    tm, d = p.shape
    sq = (p * p).astype(BF16)
    slabs = []
    for s in range(d // MXU_TILE):
        slabs.append(_dot(sq[:, s * MXU_TILE:(s + 1) * MXU_TILE], gsum_ref[...]))
    n = p * lax.rsqrt(jnp.concatenate(slabs, axis=1) + EPS)
    outs = []
    for h in range(d // LANES):
        sl = n[:, h * LANES:(h + 1) * LANES]
        up = pltpu.roll(sl, LANES - ROPE_AXIS_DIM // 2, axis=1)
        dn = pltpu.roll(sl, ROPE_AXIS_DIM // 2, axis=1)
        outs.append(sl * cos_g + jnp.where(lo_mask, up, dn) * sin_g)
    return jnp.concatenate(outs, axis=1)


def _proj_kernel(offs, x_ref, ctx_ref, shift_ref, scale_ref, wa_ref, wb_ref, wlr_ref, wdec_ref,
                 gains_ref, cos_ref, sin_ref, trif_ref, trib_ref, gsum_ref,
                 gq_ref, gk_ref, gvt_ref, cf_ref, cb_ref, gg_ref, dg_ref, mg_ref, md_ref,
                 dq_ref, dk_ref, dvt_ref):
    t = pl.program_id(1)
    latent = t > 0
    xin = jnp.where(latent, x_ref[0], ctx_ref[0])
    ms = jnp.mean(xin * xin, axis=-1, keepdims=True)
    h = xin * lax.rsqrt(ms + EPS) * (1.0 + scale_ref[0]) + shift_ref[0]
    hb = h.astype(BF16)

    def proj(name):
        which, a, b = offs[name]
        return _dot(hb, (wa_ref, wb_ref, wlr_ref)[which][:, a:b])

    lr3 = proj("lr")
    lr_hi, lr_lo = _split_bf16(lr3)
    lane = lax.broadcasted_iota(jnp.int32, lr3.shape, 1)
    lhs = jnp.where(lane < 4 * GLA_RANK, lr_hi,
                    jnp.where(lane < 6 * GLA_RANK, lr_lo, jnp.ones_like(lr_lo)))
    z = _dot(lhs, wdec_ref[...])
    la = jnp.minimum(z, 0.0) - jnp.log(1.0 + jnp.exp(-jnp.abs(z)))
    half = la.shape[1] // 2
    la_hi, la_lo = _split_bf16(la)

    gq_ref[0] = (proj("gq") * (gq_ref.shape[2] // GLA_HEADS) ** -0.5).astype(BF16)
    gk_ref[0] = proj("gk").astype(BF16)
    gvt_ref[0] = proj("gv").T.astype(BF16)

    for name, ref in (("gg", gg_ref), ("dg", dg_ref)):
        hg = proj(name)
        ref[0] = (hg * jnp.tanh(hg) + hg).astype(BF16)
    for name, ref in (("mg", mg_ref), ("md", md_ref)):
        ref[0] = (0.5 * jnp.tanh(proj(name)) + 0.5).astype(BF16)

    cf_ref[0] = _dot(trif_ref[...], la_hi[:, :half]) + _dot(trif_ref[...], la_lo[:, :half])
    cb_ref[0] = _dot(trib_ref[...], la_hi[:, half:]) + _dot(trib_ref[...], la_lo[:, half:])

    cos = cos_ref[...]
    sin_signed = sin_ref[...]
    lane = lax.broadcasted_iota(jnp.int32, cos.shape, 1)
    lo_mask = (lane % ROPE_AXIS_DIM) < (ROPE_AXIS_DIM // 2)
    dq_ref[0] = _group_norm_rope(proj("dq"), gsum_ref, cos * gains_ref[0:1], sin_signed * gains_ref[1:2],
                                 lo_mask).astype(BF16)
    dk_ref[0] = _group_norm_rope(proj("dk"), gsum_ref, cos * gains_ref[2:3], sin_signed * gains_ref[3:4],
                                 lo_mask).astype(BF16)
    dvt_ref[0] = proj("dv").T.astype(BF16)


def _proj_call(x, ctx, mod3, w_a, w_b, w_lr, offs, wdec, gains, cos, sin_signed,
               trif, trib, gsum):
    b_, l_, d = x.shape
    tm = ROW_TILE
    assert ctx.shape[1] == tm and l_ % tm == 0
    nt = l_ // tm + 1
    lt = l_ + tm
    n_ctx_row = b_
    half_k = offs["gq"][2] - offs["gq"][1]

    lat = lambda b, t: (b, jnp.maximum(t - 1, 0), 0)
    allr = lambda b, t: (b, t, 0)
    const2 = lambda b, t: (0, 0)
    once = pl.Buffered(1)

    def mod_spec(col):
        return pl.BlockSpec((1, 1, d), lambda b, t: (jnp.where(t == 0, n_ctx_row, b), 0, col))

    in_specs = [
        pl.BlockSpec((1, tm, d), lat),
        pl.BlockSpec((1, tm, d), lambda b, t: (b, 0, 0)),
        mod_spec(0), mod_spec(1),
        pl.BlockSpec(w_a.shape, const2, pipeline_mode=once),
        pl.BlockSpec(w_b.shape, const2, pipeline_mode=once),
        pl.BlockSpec(w_lr.shape, const2, pipeline_mode=once),
        pl.BlockSpec(wdec.shape, const2, pipeline_mode=once),
        pl.BlockSpec(gains.shape, const2, pipeline_mode=once),
        pl.BlockSpec((tm, LANES), lambda b, t: (t, 0)),
        pl.BlockSpec((tm, LANES), lambda b, t: (t, 0)),
        pl.BlockSpec(trif.shape, const2, pipeline_mode=once),
        pl.BlockSpec(trib.shape, const2, pipeline_mode=once),
        pl.BlockSpec(gsum.shape, const2, pipeline_mode=once),
    ]
    out_specs = [
        pl.BlockSpec((1, tm, half_k), lat),
        pl.BlockSpec((1, tm, half_k), allr),
        pl.BlockSpec((1, d, tm), lambda b, t: (b, 0, t)),
        pl.BlockSpec((1, tm, half_k), allr),
        pl.BlockSpec((1, tm, half_k), allr),
        pl.BlockSpec((1, tm, d), lat),
        pl.BlockSpec((1, tm, d), lat),
        pl.BlockSpec((1, tm, d), lat),
        pl.BlockSpec((1, tm, d), lat),
        pl.BlockSpec((1, tm, d), lat),
        pl.BlockSpec((1, tm, d), allr),
        pl.BlockSpec((1, d, tm), lambda b, t: (b, 0, t)),
    ]
    sd = jax.ShapeDtypeStruct
    out_shape = [
        sd((b_, l_, half_k), BF16), sd((b_, lt, half_k), BF16), sd((b_, d, lt), BF16),
        sd((b_, lt, half_k), F32), sd((b_, lt, half_k), F32),
        sd((b_, l_, d), BF16), sd((b_, l_, d), BF16), sd((b_, l_, d), BF16), sd((b_, l_, d), BF16),
        sd((b_, l_, d), BF16), sd((b_, lt, d), BF16), sd((b_, d, lt), BF16),
    ]
    return pl.pallas_call(
        functools.partial(_proj_kernel, offs),
        grid=(b_, nt),
        in_specs=in_specs, out_specs=out_specs, out_shape=out_shape,
        compiler_params=pltpu.CompilerParams(dimension_semantics=("arbitrary", "arbitrary"),
                                             vmem_limit_bytes=VMEM_LIMIT),
        name="proj",
    )(x, ctx, mod3, mod3, w_a, w_b, w_lr, wdec, gains, cos, sin_signed, trif, trib, gsum)


def _chunk_rows(rows, dk):
    return jnp.concatenate([jnp.broadcast_to(r, (GLA_CHUNK, dk)) for r in rows], axis=0)


def _gla_direction(forward, q_ref, k_ref, vt_ref, c_ref, mask_ref, o_ref, st_ref):
    tm = k_ref.shape[1]
    nh = GLA_HEADS
    dk = k_ref.shape[2] // nh
    dv = vt_ref.shape[1] // nh
    nc = tm // GLA_CHUNK
    half = GLA_CHUNK // 2
    mask = mask_ref[...] != 0.0
    row_chunk = lax.broadcasted_iota(jnp.int32, (tm, dk), 0) // GLA_CHUNK
    order = list(range(nc)) if forward else list(reversed(range(nc)))
    zrow = jnp.zeros((1, dk), F32)
    for h in range(nh):
        ks = slice(h * dk, (h + 1) * dk)
        vs = slice(h * dv, (h + 1) * dv)
        cum = c_ref[0, :, ks]
        q = q_ref[0, :, ks].astype(F32)
        k = k_ref[0, :, ks].astype(F32)
        vt = vt_ref[0, vs, :]
        last, ref = [], []
        for c in range(nc):
            base = c * GLA_CHUNK
            li = base + GLA_CHUNK - 1 if forward else base
            ce = base + half - 1 if forward else base + half
            last.append(cum[li:li + 1])
            ref.append(cum[ce:ce + 1])
        lastb = _chunk_rows(last, dk)
        refb = _chunk_rows(ref, dk)
        qe = (q * jnp.exp(cum - refb)).astype(BF16)
        ke = (k * jnp.exp(refb - cum)).astype(BF16)
        a = jnp.where(mask, _dot_nt(qe, ke), 0.0).astype(BF16)
        o_intra = _dot_nt(a, vt)
        p_start, p_end, run = {}, {}, zrow
        for c in order:
            p_start[c] = run
            run = run + last[c]
            p_end[c] = run
        p_total = run
        qd = q * jnp.exp(cum)
        kd = k * jnp.exp(lastb - cum)
        sources = order[:-1]
        k_cols = [jnp.where(row_chunk == s, kd, 0.0).astype(BF16) for s in sources]
        k_cols.append((kd * jnp.exp(_chunk_rows([p_total - p_end[c] for c in range(nc)], dk))).astype(BF16))
        ut = _dot(vt, jnp.concatenate(k_cols, axis=1))
        st = st_ref[h]
        wt = jnp.concatenate([st, ut[:, :(nc - 1) * dk]], axis=1).astype(BF16)
        q_cols = [(qd * jnp.exp(_chunk_rows([p_start[c] for c in range(nc)], dk))).astype(BF16)]
        for i, s in enumerate(sources):
            later = order[i + 1:]
            fac = [jnp.exp(p_start[c] - p_end[s]) if c in later else zrow for c in range(nc)]
            q_cols.append((qd * _chunk_rows(fac, dk)).astype(BF16))
        o_inter = _dot_nt(jnp.concatenate(q_cols, axis=1), wt)
        st_ref[h] = st * jnp.exp(p_total) + ut[:, (nc - 1) * dk:]
        o_ref[0, :, vs] = (o_intra + o_inter).astype(o_ref.dtype)


def _gla_kernel(qf_ref, kf_ref, vtf_ref, cf_ref, qb_ref, kb_ref, vtb_ref, cb_ref, maskf_ref, maskb_ref,
                of_ref, ob_ref, sf_ref, sb_ref):
    @pl.when(pl.program_id(1) == 0)
    def _():
        sf_ref[...] = jnp.zeros_like(sf_ref)
        sb_ref[...] = jnp.zeros_like(sb_ref)

    _gla_direction(True, qf_ref, kf_ref, vtf_ref, cf_ref, maskf_ref, of_ref, sf_ref)
    _gla_direction(False, qb_ref, kb_ref, vtb_ref, cb_ref, maskb_ref, ob_ref, sb_ref)


def _gla_call(gq, gk, gvt, cumf, cumb, maskf, maskb):
    b_, l_, kw = gq.shape
    lt = gk.shape[1]
    d = gvt.shape[1]
    tm = ROW_TILE
    nt = lt // tm
    nlat = l_ // tm
    dk = kw // GLA_HEADS
    dv = d // GLA_HEADS

    bwd_blk = lambda s: jnp.where(s == 0, 0, nt - s)
    f_lat = lambda b, s: (b, jnp.maximum(s - 1, 0), 0)
    f_all = lambda b, s: (b, s, 0)
    b_lat = lambda b, s: (b, jnp.where(s == 0, nlat - 1, nlat - s), 0)
    b_all = lambda b, s: (b, bwd_blk(s), 0)
    return pl.pallas_call(
        _gla_kernel,
        grid=(b_, nt),
        in_specs=[
            pl.BlockSpec((1, tm, kw), f_lat), pl.BlockSpec((1, tm, kw), f_all),
            pl.BlockSpec((1, d, tm), lambda b, s: (b, 0, s)),
            pl.BlockSpec((1, tm, kw), f_all),
            pl.BlockSpec((1, tm, kw), b_lat), pl.BlockSpec((1, tm, kw), b_all),
            pl.BlockSpec((1, d, tm), lambda b, s: (b, 0, bwd_blk(s))),
            pl.BlockSpec((1, tm, kw), b_all),
            pl.BlockSpec((tm, tm), lambda b, s: (0, 0), pipeline_mode=pl.Buffered(1)),
            pl.BlockSpec((tm, tm), lambda b, s: (0, 0), pipeline_mode=pl.Buffered(1)),
        ],
        out_specs=[pl.BlockSpec((1, tm, d), f_lat), pl.BlockSpec((1, tm, d), b_lat)],
        out_shape=[jax.ShapeDtypeStruct((b_, l_, d), BF16), jax.ShapeDtypeStruct((b_, l_, d), BF16)],
        scratch_shapes=[pltpu.VMEM((GLA_HEADS, dv, dk), F32), pltpu.VMEM((GLA_HEADS, dv, dk), F32)],
        compiler_params=pltpu.CompilerParams(dimension_semantics=("arbitrary", "arbitrary"),
                                             vmem_limit_bytes=VMEM_LIMIT),
        name="gla",
    )(gq, gk, gvt, cumf, gq, gk, gvt, cumb, maskf, maskb)


def _attn_kernel(tk, scal_ref, q_ref, k_ref, vt_ref, o_ref, acc1, acc2, m1, l1, m2, l2):
    nsub = acc1.shape[0]
    tq = q_ref.shape[1] // nsub
    nk = k_ref.shape[1] // tk
    lam = scal_ref[0]

    def streams(sub):
        q = q_ref[0, sub * tq:(sub + 1) * tq, :]
        lane = lax.broadcasted_iota(jnp.int32, q.shape, 1)
        zero = jnp.zeros_like(q)
        q1 = jnp.where(lane < DIFF_DK, q, zero)
        q2 = jnp.where(lane >= DIFF_DK, q, zero)
        return ((q1, acc1.at[sub], m1.at[sub], l1.at[sub]), (q2, acc2.at[sub], m2.at[sub], l2.at[sub]))

    def finish(sub):
        inv1 = 1.0 / jnp.sum(l1[sub], axis=0, keepdims=True)
        inv2 = 1.0 / jnp.sum(l2[sub], axis=0, keepdims=True)
        o = acc1[sub] * inv1 - lam * (acc2[sub] * inv2)
        o = o * lax.rsqrt(jnp.mean(o * o, axis=0, keepdims=True) + EPS)
        o_ref[0, :, sub * tq:(sub + 1) * tq] = o.astype(o_ref.dtype)

    def bounded_tiles(sub):
        st = streams(sub)

        def scores(j):
            kk = k_ref[0, j * tk:(j + 1) * tk, :]
            return [_dot_nt(kk, qq) for qq, _, _, _ in st]

        s_next = scores(0)
        for j in range(nk):
            s_cur = s_next
            if j + 1 < nk:
                s_next = scores(j + 1)
            vt = vt_ref[0, :, j * tk:(j + 1) * tk]
            for s, (_, acc, _, l) in zip(s_cur, st):
                p = jnp.exp2(s)
                l_part = jnp.sum(p.reshape(tk // SUBLANES, SUBLANES, tq), axis=0)
                pv = _dot(vt, p.astype(BF16))
                l[...] = l_part if j == 0 else l[...] + l_part
                acc[...] = pv if j == 0 else acc[...] + pv

    def online_tiles(sub):
        st = streams(sub)
        for _, acc, m, l in st:
            acc[...] = jnp.zeros_like(acc)
            l[...] = jnp.zeros_like(l)
            m[...] = jnp.full_like(m, -1e30)

        def body(j, carry):
            off = pl.multiple_of(j * tk, tk)
            kk = k_ref[0, pl.ds(off, tk), :]
            vt = vt_ref[0, :, pl.ds(off, tk)]
            for qq, acc, m, l in st:
                s = _dot_nt(kk, qq)
                m_new = jnp.maximum(m[...], jnp.max(s, axis=0, keepdims=True))
                alpha = jnp.exp2(m[...] - m_new)
                p = jnp.exp2(s - m_new)
                l[0:1] = alpha * l[0:1] + jnp.sum(p, axis=0, keepdims=True)
                acc[...] = alpha * acc[...] + _dot(vt, p.astype(BF16))
                m[...] = m_new
            return carry

        lax.fori_loop(0, nk, body, 0)

    bounded = scal_ref[1] != 0.0

    @pl.when(bounded)
    def _():
        for sub in range(nsub):
            bounded_tiles(sub)
            finish(sub)

    @pl.when(jnp.logical_not(bounded))
    def _():
        for sub in range(nsub):
            online_tiles(sub)
            finish(sub)


def _attn_call(scal, dq, dk, dvt, tq, tk, nsub):
    b_, l_, d = dq.shape
    lt = dk.shape[1]
    nh = d // LANES
    tstep = tq * nsub
    assert l_ % tstep == 0 and lt % tk == 0
    return pl.pallas_call(
        functools.partial(_attn_kernel, tk),
        grid=(b_, nh, l_ // tstep),
        in_specs=[
            pl.BlockSpec(memory_space=pltpu.SMEM),
            pl.BlockSpec((1, tstep, LANES), lambda b, h, i: (b, i, h)),
            pl.BlockSpec((1, lt, LANES), lambda b, h, i: (b, 0, h)),
            pl.BlockSpec((1, LANES, lt), lambda b, h, i: (b, h, 0)),
        ],
        out_specs=pl.BlockSpec((1, LANES, tstep), lambda b, h, i: (b, h, i)),
        out_shape=jax.ShapeDtypeStruct((b_, d, l_), BF16),
        scratch_shapes=[pltpu.VMEM((nsub, LANES, tq), F32), pltpu.VMEM((nsub, LANES, tq), F32),
                        pltpu.VMEM((nsub, 1, tq), F32), pltpu.VMEM((nsub, SUBLANES, tq), F32),
                        pltpu.VMEM((nsub, 1, tq), F32), pltpu.VMEM((nsub, SUBLANES, tq), F32)],
        compiler_params=pltpu.CompilerParams(
            dimension_semantics=("arbitrary", "arbitrary", "arbitrary"),
            vmem_limit_bytes=VMEM_LIMIT),
        name="diffattn",
    )(scal, dq, dk, dvt)


def _combine_kernel(x_ref, g_ref, of_ref, ob_ref, od_ref, gg_ref, dg_ref, mg_ref, md_ref,
                    ggain_ref, dgain_ref, wg_ref, wd_ref, wo_ref, out_ref):
    d = x_ref.shape[2]
    dv = d // GLA_HEADS
    og = of_ref[0].astype(F32) + ob_ref[0].astype(F32)
    normed = []
    for h in range(GLA_HEADS):
        sl = og[:, h * dv:(h + 1) * dv]
        normed.append(sl * lax.rsqrt(jnp.mean(sl * sl, axis=-1, keepdims=True) + EPS))
    a = jnp.concatenate(normed, axis=1) * ggain_ref[...] * gg_ref[0].astype(F32)
    bb = od_ref[0].astype(F32).T * dgain_ref[...] * dg_ref[0].astype(F32)
    y = (mg_ref[0].astype(F32) * _dot(a.astype(BF16), wg_ref[...])
         + md_ref[0].astype(F32) * _dot(bb.astype(BF16), wd_ref[...]))
    out_ref[0] = x_ref[0] + g_ref[0] * _dot(y.astype(BF16), wo_ref[...])


def _combine_call(x, mod3, o_f, o_b, o_d, gg, dg, mg, md, ggain, dgain, wg, wd, wo, tm):
    b_, l_, d = x.shape
    row = lambda b, i: (b, i, 0)
    const2 = lambda b, i: (0, 0)
    once = pl.Buffered(1)
    big = pl.BlockSpec((1, tm, d), row)
    wspec = pl.BlockSpec((d, d), const2, pipeline_mode=once)
    return pl.pallas_call(
        _combine_kernel,
        grid=(b_, l_ // tm),
        in_specs=[big, pl.BlockSpec((1, 1, d), lambda b, i: (b, 0, 2)),
                  big, big, pl.BlockSpec((1, d, tm), lambda b, i: (b, 0, i)), big, big, big, big,
                  pl.BlockSpec((1, d), const2, pipeline_mode=once),
                  pl.BlockSpec((1, d), const2, pipeline_mode=once),
                  wspec, wspec, wspec],
        out_specs=big,
        out_shape=jax.ShapeDtypeStruct(x.shape, x.dtype),
        compiler_params=pltpu.CompilerParams(dimension_semantics=("arbitrary", "arbitrary"),
                                             vmem_limit_bytes=VMEM_LIMIT),
        name="combine",
    )(x, mod3, o_f, o_b, o_d, gg, dg, mg, md, ggain, dgain, wg, wd, wo)


def _rope_tables(n_tokens, n_ctx):
    f32 = np.float32
    rows = n_tokens // GRID_W
    inv_freq = f32(ROPE_BASE) ** (-np.arange(0, ROPE_AXIS_DIM, 2, dtype=f32) / f32(ROPE_AXIS_DIM))
    ang_r = (np.arange(rows, dtype=f32)[:, None] * inv_freq).astype(f32)
    ang_c = (np.arange(GRID_W, dtype=f32)[:, None] * inv_freq).astype(f32)
    sign = np.where(np.arange(ROPE_AXIS_DIM) < ROPE_AXIS_DIM // 2, -1.0, 1.0)
    two = lambda t: np.concatenate([t, t], axis=-1).astype(f32)

    def table(fn, sgn):
        tr = jnp.broadcast_to(jnp.asarray(two(fn(ang_r)) * sgn, F32)[:, None, :], (rows, GRID_W, ROPE_AXIS_DIM))
        tc = jnp.broadcast_to(jnp.asarray(two(fn(ang_c)) * sgn, F32)[None, :, :], (rows, GRID_W, ROPE_AXIS_DIM))
        return jnp.concatenate([tr, tc, tr, tc], axis=-1).reshape(n_tokens, LANES)

    cos = jnp.concatenate([jnp.ones((n_ctx, LANES), F32), table(np.cos, 1.0)], axis=0)
    sin = jnp.concatenate([jnp.zeros((n_ctx, LANES), F32), table(np.sin, sign)], axis=0)
    return cos, sin


def _block_tri(n, upper):
    i = np.arange(n)
    same = (i[:, None] // GLA_CHUNK) == (i[None, :] // GLA_CHUNK)
    tri = (i[None, :] >= i[:, None]) if upper else (i[:, None] >= i[None, :])
    return jnp.asarray((same & tri).astype(np.float32)).astype(BF16)


def _attn_tiles(l_, lt):
    tq = 1024 if l_ % 1024 == 0 else l_
    nsub = 2 if l_ % (2 * tq) == 0 else 1
    for tk in (768, 512, 384, 256, 128):
        if lt % tk == 0:
            return tq, tk, nsub
    raise ValueError("key length must be a multiple of 128")


def kernel(x, c, ctx, c_ctx, w_ada, b_ada, w_in, gla_w_decay, gla_b_decay, gla_norm, diff_q_norm,
           diff_k_norm, diff_lambda, diff_norm, w_br_gla, w_br_diff, w_out):
    b_, l_, d = x.shape
    n_ctx = ctx.shape[1]
    depth = w_ada.shape[0]
    assert depth == 1, "single-layer block"
    layer = 0
    lam_init = 0.8 - 0.6 * math.exp(-0.3 * layer)
    half_k = d // 2

    n_mod = -(-(b_ + 1) // SUBLANES) * SUBLANES
    cc = jnp.zeros((n_mod, d), F32).at[:b_].set(c).at[b_].set(c_ctx)
    mod, lam_tile = _mod_call(cc, w_ada[layer], b_ada[layer][None, :], diff_lambda[layer], lam_init)
    mod3 = mod.reshape(n_mod, 1, 3 * d)
    lam = lam_tile[0, :1]

    w = w_in[layer]
    sizes = (half_k, half_k, d, d, 2 * GLA_RANK, d, d, d, d, d, d)
    names = ("gq", "gk", "gv", "gg", "lr", "dq", "dk", "dv", "dg", "mg", "md")
    starts = np.concatenate([[0], np.cumsum(sizes)])
    lr_at = names.index("lr")
    lr_lo, lr_hi = int(starts[lr_at]), int(starts[lr_at + 1])
    col_scale = np.ones((1, int(starts[-1])), np.float32)
    for name in ("gg", "dg", "mg", "md"):
        i = names.index(name)
        col_scale[:, int(starts[i]):int(starts[i + 1])] = 0.5
    col_scale = jnp.asarray(col_scale, BF16)
    w16 = w.astype(BF16)
    w_a = w16[:, :lr_lo] * col_scale[:, :lr_lo]
    w_b = w16[:, lr_hi:] * col_scale[:, lr_hi:]
    w_lr = jnp.concatenate([w[:, lr_lo:lr_hi]] * 3 + [jnp.zeros((d, LANES - 6 * GLA_RANK), w.dtype)],
                           axis=1).astype(BF16)
    offs = {"lr": (2, 0, LANES)}
    for i, name in enumerate(names):
        if i < lr_at:
            offs[name] = (0, int(starts[i]), int(starts[i + 1]))
        elif i > lr_at:
            offs[name] = (1, int(starts[i]) - lr_hi, int(starts[i + 1]) - lr_hi)

    zk = jnp.zeros((GLA_RANK, half_k), F32)
    wd_f32 = jnp.concatenate([jnp.concatenate([gla_w_decay[layer, 0], zk], axis=1),
                              jnp.concatenate([zk, gla_w_decay[layer, 1]], axis=1)], axis=0)
    bd_f32 = jnp.concatenate([gla_b_decay[layer, 0], gla_b_decay[layer, 1]])[None, :]
    hi_lo = lambda t: (t.astype(BF16), (t - t.astype(BF16).astype(F32)).astype(BF16))
    (wd_hi, wd_lo), (bd_hi, bd_lo) = hi_lo(wd_f32), hi_lo(bd_f32)
    wdec = jnp.concatenate([wd_hi, wd_lo, wd_hi, bd_hi, bd_lo,
                            jnp.zeros((LANES - 6 * GLA_RANK - 2, 2 * half_k), BF16)], axis=0)

    cos, sin_signed = _rope_tables(l_, n_ctx)
    lane = np.arange(LANES)
    partner = np.where(lane % ROPE_AXIS_DIM < ROPE_AXIS_DIM // 2,
                       lane + ROPE_AXIS_DIM // 2, lane - ROPE_AXIS_DIM // 2)
    g_q = jnp.tile(diff_q_norm[layer], LANES // DIFF_DK) * (DIFF_DK ** -0.5 * LOG2E)
    g_k = jnp.tile(diff_k_norm[layer], LANES // DIFF_DK)
    gain_rows = [g_q, g_q[partner], g_k, g_k[partner]]
    gains = jnp.stack(gain_rows + [jnp.zeros((LANES,), F32)] * (SUBLANES - len(gain_rows)))

    trif = _block_tri(ROW_TILE, upper=False) * (1.0 / GLA_TAU)
    trib = _block_tri(ROW_TILE, upper=True) * (1.0 / GLA_TAU)
    gi = np.arange(MXU_TILE)
    gsum = jnp.asarray((gi[:, None] // DIFF_DK == gi[None, :] // DIFF_DK).astype(np.float32)
                       / DIFF_DK).astype(BF16)

    (gq, gk, gvt, cumf, cumb, gg, dg, mg, md, dq, dk, dvt) = _proj_call(
        x, ctx, mod3, w_a, w_b, w_lr, offs, wdec, gains, cos, sin_signed, trif, trib, gsum)

    o_f, o_b = _gla_call(gq, gk, gvt, cumf, cumb, _block_tri(ROW_TILE, upper=False).astype(F32),
                         _block_tri(ROW_TILE, upper=True).astype(F32))

    tq, tk, nsub = _attn_tiles(l_, l_ + n_ctx)
    score_bound = DIFF_DK ** 0.5 * jnp.max(jnp.abs(diff_q_norm[layer])) * jnp.max(jnp.abs(diff_k_norm[layer]))
    bounded = (score_bound * BOUND_MARGIN <= SCORE_BOUND).astype(F32)
    scal = jnp.concatenate([lam, bounded[None]])
    o_d = _attn_call(scal, dq, dk, dvt, tq, tk, nsub)

    ggain = jnp.tile(gla_norm[layer], GLA_HEADS)[None, :]
    dgain = (jnp.tile(diff_norm[layer], d // diff_norm.shape[1]) * (1.0 - lam_init))[None, :]
    tm = 512 if l_ % 512 == 0 else l_
    return _combine_call(x, mod3, o_f, o_b, o_d, gg, dg, mg, md, ggain, dgain,
                         w_br_gla[layer].astype(BF16), w_br_diff[layer].astype(BF16),
                         w_out[layer].astype(BF16), tm)
```
